```python
import math
import jax, jax.numpy as jnp
from jax import lax
import numpy as np

D_MODEL = 1024
BATCH = 8
SEQ = 4096
DEPTH = 2
DEC_BATCH = 2
DEC_SEQ = 8192
PAST_LEN = 128

HEAD_DIM = 64
A_HEADS = D_MODEL // (2 * HEAD_DIM)
DILATION_GROUPS = ((128, 1), (512, 4), (2048, 16))
A_BLOCK = 64
B_Q_HEADS = D_MODEL // (2 * HEAD_DIM)
B_KV_HEADS = B_Q_HEADS // 4
B_HALF_WINDOW = 128
B_BLOCK = 128
C_HEADS = D_MODEL // (2 * HEAD_DIM)
C_QBLOCK = 128
N_GROUPS = 4
EXPERTS_PER_GROUP = 8
N_EXPERTS = N_GROUPS * EXPERTS_PER_GROUP
TOP_K = 2
D_EXPERT = D_MODEL // 2
ROUTE_CHUNK = 256
ROPE_THETA = 10000.0
EPS = 1e-6
NEG_INF = -1e30
N_EVEN = (DEPTH + 1) // 2
N_ODD = DEPTH // 2
EVEN_IN = (3 * A_HEADS + B_Q_HEADS + 2 * B_KV_HEADS) * HEAD_DIM
EVEN_MIX = (A_HEADS + B_Q_HEADS) * HEAD_DIM
ODD_IN = 6 * C_HEADS * HEAD_DIM
ODD_MIX = 2 * C_HEADS * HEAD_DIM

kernel_name = 'hybrid_dilated_window_diff_hmoe_encoder'


def rms_norm(x, g):
    xf = x.astype(jnp.float32)
    y = xf * lax.rsqrt(jnp.mean(xf * xf, axis=-1, keepdims=True) + EPS)
    return (y * g.astype(jnp.float32)).astype(x.dtype)


def rope(x, pos):
    half = x.shape[-1] // 2
    inv = ROPE_THETA ** (-jnp.arange(half, dtype=jnp.float32) / half)
    ang = pos.astype(jnp.float32)[:, None] * inv[None, :]
    cos = jnp.cos(ang)[None, :, None, :]
    sin = jnp.sin(ang)[None, :, None, :]
    xf = x.astype(jnp.float32)
    x1, x2 = xf[..., :half], xf[..., half:]
    return jnp.concatenate([x1 * cos - x2 * sin, x2 * cos + x1 * sin], axis=-1).astype(x.dtype)


def banded_attention(q, k, v, half_window, blk, sink=None):
    N, L, Hk, G, D = q.shape
    nb = -(-L // blk)
    Lp = nb * blk
    pad = Lp - L
    qb = jnp.pad(q, ((0, 0), (0, pad), (0, 0), (0, 0), (0, 0))).reshape(N, nb, blk, Hk, G, D)
    kb = jnp.pad(k, ((0, 0), (blk, pad + blk), (0, 0), (0, 0))).reshape(N, nb + 2, blk, Hk, D)
    vb = jnp.pad(v, ((0, 0), (blk, pad + blk), (0, 0), (0, 0))).reshape(N, nb + 2, blk, Hk, D)
    kw = jnp.concatenate([kb[:, :-2], kb[:, 1:-1], kb[:, 2:]], axis=2)
    vw = jnp.concatenate([vb[:, :-2], vb[:, 1:-1], vb[:, 2:]], axis=2)
    s = jnp.einsum('nbqhgd,nbkhd->nbhgqk', qb, kw, preferred_element_type=jnp.float32) * (D ** -0.5)
    qpos = jnp.arange(nb)[:, None] * blk + jnp.arange(blk)[None, :]
    kpos = jnp.arange(nb)[:, None] * blk - blk + jnp.arange(3 * blk)[None, :]
    rel = kpos[:, None, :] - qpos[:, :, None]
    valid = (jnp.abs(rel) <= half_window) & (kpos[:, None, :] >= 0) & (kpos[:, None, :] < L)
    s = jnp.where(valid[None, :, None, None], s, NEG_INF)
    m = jnp.max(s, axis=-1)
    if sink is not None:
        sk = sink.astype(jnp.float32)[None, None, :, :, None]
        m = jnp.maximum(m, sk)
    p = jnp.exp(s - m[..., None])
    den = jnp.sum(p, axis=-1)
    if sink is not None:
        den = den + jnp.exp(sk - m)
    o = jnp.einsum('nbhgqk,nbkhd->nbqhgd', p.astype(v.dtype), vw, preferred_element_type=jnp.float32)
    o = o / jnp.moveaxis(den, -1, 2)[..., None]
    lse = jnp.moveaxis(m + jnp.log(den), -1, 2)
    return o.reshape(N, Lp, Hk, G, D)[:, :L], lse.reshape(N, Lp, Hk, G)[:, :L]


def dilated_attention(q, k, v, window, dilation):
    B, S, H, D = q.shape
    L = S // dilation

    def to_classes(t):
        return t.reshape(B, L, dilation, H, D).transpose(0, 2, 1, 3, 4).reshape(B * dilation, L, H, D)

    o, lse = banded_attention(to_classes(q)[:, :, :, None], to_classes(k), to_classes(v),
                              window // (2 * dilation), A_BLOCK)
    o = o[:, :, :, 0].reshape(B, dilation, L, H, D).transpose(0, 2, 1, 3, 4).reshape(B, S, H, D)
    lse = lse[..., 0].reshape(B, dilation, L, H).transpose(0, 2, 1, 3).reshape(B, S, H)
    return o, lse


def even_mixer(h, pos, w_in, qn_a, kn_a, qn_b, kn_b, sink, w_out):
    B, S, _ = h.shape
    hd = HEAD_DIM
    sizes = [A_HEADS * hd] * 3 + [B_Q_HEADS * hd, B_KV_HEADS * hd, B_KV_HEADS * hd]
    qa, ka, va, qb, kb, vb = jnp.split(h @ w_in, np.cumsum(sizes)[:-1].tolist(), axis=-1)
    qa = rope(rms_norm(qa.reshape(B, S, A_HEADS, hd), qn_a), pos)
    ka = rope(rms_norm(ka.reshape(B, S, A_HEADS, hd), kn_a), pos)
    va = va.reshape(B, S, A_HEADS, hd)
    outs, lses = [], []
    for window, dil in DILATION_GROUPS:
        o, lse = dilated_attention(qa, ka, va, window, dil)
        outs.append(o)
        lses.append(lse)
    wts = jax.nn.softmax(jnp.stack(lses), axis=0)
    oa = jnp.sum(wts[..., None] * jnp.stack(outs), axis=0)
    qb = rope(rms_norm(qb.reshape(B, S, B_Q_HEADS, hd), qn_b), pos)
    qb = qb.reshape(B, S, B_KV_HEADS, B_Q_HEADS // B_KV_HEADS, hd)
    kb = rope(rms_norm(kb.reshape(B, S, B_KV_HEADS, hd), kn_b), pos)
    vb = vb.reshape(B, S, B_KV_HEADS, hd)
    ob, _ = banded_attention(qb, kb, vb, B_HALF_WINDOW, B_BLOCK, sink.reshape(B_KV_HEADS, -1))
    mixed = jnp.concatenate([oa.reshape(B, S, -1), ob.reshape(B, S, -1)], axis=-1).astype(h.dtype)
    return mixed @ w_out


def diff_mixer(h, pos, w_in, qn, kn, lq1, lk1, lq2, lk2, subln, w_out, lam_init):
    B, S, _ = h.shape
    hd = HEAD_DIM
    f32 = jnp.float32
    q, k, v = jnp.split(h @ w_in, [2 * C_HEADS * hd, 4 * C_HEADS * hd], axis=-1)
    q = rope(rms_norm(q.reshape(B, S, 2 * C_HEADS, hd), qn), pos).reshape(B, S, C_HEADS, 2, hd)
    k = rope(rms_norm(k.reshape(B, S, 2 * C_HEADS, hd), kn), pos).reshape(B, S, C_HEADS, 2, hd)
    v = v.reshape(B, S, C_HEADS, 2 * hd)
    lam = (jnp.exp(jnp.sum(lq1.astype(f32) * lk1.astype(f32)))
           - jnp.exp(jnp.sum(lq2.astype(f32) * lk2.astype(f32))) + lam_init)
    scale = hd ** -0.5
    qblocks = q.reshape(B, S // C_QBLOCK, C_QBLOCK, C_HEADS, 2, hd).transpose(1, 0, 2, 3, 4, 5)

    def block(qb):
        s = jnp.einsum('bqhcd,bkhcd->bhcqk', qb, k, preferred_element_type=f32) * scale
        a = jax.nn.softmax(s, axis=-1)
        w = a[:, :, 0] - lam * a[:, :, 1]
        return jnp.einsum('bhqk,bkhe->bqhe', w.astype(v.dtype), v, preferred_element_type=f32)

    o = lax.map(block, qblocks)
    o = o.transpose(1, 0, 2, 3, 4).reshape(B, S, C_HEADS, 2 * hd)
    o = rms_norm(o, subln) * (1.0 - lam_init)
    return o.reshape(B, S, -1).astype(h.dtype) @ w_out


def routed_experts(xt, eid, gate, w_gate, w_up, w_down):
    T, D = xt.shape
    M = T * TOP_K
    flat_e = eid.reshape(M)
    order = jnp.argsort(flat_e)
    se = flat_e[order]
    stok = order // TOP_K
    sgate = gate.reshape(M)[order]
    nch = -(-M // ROUTE_CHUNK)
    Mp = nch * ROUTE_CHUNK
    xs = jnp.pad(xt[stok], ((0, Mp - M), (0, 0)))
    se_p = jnp.pad(se, (0, Mp - M), constant_values=N_EXPERTS)
    counts = jnp.bincount(flat_e, length=N_EXPERTS)
    ends = jnp.cumsum(counts)
    starts = ends - counts
    first = starts // ROUTE_CHUNK
    last = jnp.maximum(ends - 1, 0) // ROUTE_CHUNK
    n_items = jnp.where(counts > 0, last - first + 1, 0)
    item_end = jnp.cumsum(n_items)
    item_start = item_end - n_items
    P = nch + N_EXPERTS - 1
    p = jnp.arange(P)
    e_p = jnp.minimum(jnp.searchsorted(item_end, p, side='right'), N_EXPERTS - 1)
    c_p = jnp.clip(first[e_p] + p - item_start[e_p], 0, nch - 1)
    live = p < item_end[-1]

    def run(args):
        e, c, ok = args
        rows = lax.dynamic_slice_in_dim(xs, c * ROUTE_CHUNK, ROUTE_CHUNK, 0)
        mask = (lax.dynamic_slice_in_dim(se_p, c * ROUTE_CHUNK, ROUTE_CHUNK, 0) == e) & ok
        y = (jax.nn.silu(rows @ w_gate[e]) * (rows @ w_up[e])) @ w_down[e]
        return jnp.where(mask[:, None], y, jnp.zeros_like(y))

    ys = lax.map(run, (e_p, c_p, live))
    row_idx = (c_p[:, None] * ROUTE_CHUNK + jnp.arange(ROUTE_CHUNK)[None, :]).reshape(-1)
    ysorted = jnp.zeros((Mp, D), ys.dtype).at[row_idx].add(ys.reshape(-1, D))
    return jax.ops.segment_sum(ysorted[:M] * sgate[:, None].astype(ys.dtype), stok, num_segments=T)


def hier_moe(h, rg_w, rg_b, re_w, re_b, w_gate, w_up, w_down):
    B, S, D = h.shape
    xt = h.reshape(B * S, D)
    lg = (xt @ rg_w).astype(jnp.float32) + rg_b.astype(jnp.float32)
    pg = jax.nn.softmax(lg, axis=-1)
    g_sel = jnp.argmax(lg, axis=-1)
    p_sel = jnp.take_along_axis(pg, g_sel[:, None], axis=1)[:, 0]
    le = jnp.einsum('td,gde->tge', xt, re_w).astype(jnp.float32) + re_b.astype(jnp.float32)
    le_sel = jnp.take_along_axis(le, g_sel[:, None, None], axis=1)[:, 0]
    top_v, top_i = lax.top_k(le_sel, TOP_K)
    gate = jax.nn.softmax(top_v, axis=-1) * p_sel[:, None]
    eid = g_sel[:, None] * EXPERTS_PER_GROUP + top_i
    y = routed_experts(xt, eid, gate, w_gate, w_up, w_down)
    return y.reshape(B, S, D).astype(h.dtype)


def trunk(x, ev_norm, ev_w_in, ev_qn_a, ev_kn_a, ev_qn_b, ev_kn_b, ev_sink, ev_w_out,
          od_norm, od_w_in, od_qn, od_kn, od_lq1, od_lk1, od_lq2, od_lk2, od_subln, od_w_out,
          ffn_norm, rg_w, rg_b, re_w, re_b, w_gate, w_up, w_down):
    pos = jnp.arange(x.shape[1])
    for layer in range(DEPTH):
        i = layer // 2
        if layer % 2 == 0:
            x = x + even_mixer(rms_norm(x, ev_norm[i]), pos, ev_w_in[i], ev_qn_a[i], ev_kn_a[i],
                               ev_qn_b[i], ev_kn_b[i], ev_sink[i], ev_w_out[i])
        else:
            lam_init = 0.8 - 0.6 * math.exp(-0.3 * layer)
            x = x + diff_mixer(rms_norm(x, od_norm[i]), pos, od_w_in[i], od_qn[i], od_kn[i],
                               od_lq1[i], od_lk1[i], od_lq2[i], od_lk2[i], od_subln[i], od_w_out[i], lam_init)
        x = x + hier_moe(rms_norm(x, ffn_norm[layer]), rg_w[layer], rg_b[layer], re_w[layer], re_b[layer],
                         w_gate[layer], w_up[layer], w_down[layer])
    return x


def setup_inputs(seed: int = 0) -> dict:
    key = jax.random.key(seed)
    keys = iter(jax.random.split(key, 40))

    def normal(shape, scale):
        return jax.random.normal(next(keys), shape, jnp.float32) * scale

    def gain(shape):
        return 1.0 + normal(shape, 0.02)

    hd = HEAD_DIM
    return {
        'x_prompt': normal((BATCH, SEQ, D_MODEL), 1.0),
        'x_sample': normal((DEC_BATCH, DEC_SEQ, D_MODEL), 1.0),
        'ev_norm': gain((N_EVEN, D_MODEL)),
        'ev_w_in': normal((N_EVEN, D_MODEL, EVEN_IN), D_MODEL ** -0.5),
        'ev_qn_a': gain((N_EVEN, hd)),
        'ev_kn_a': gain((N_EVEN, hd)),
        'ev_qn_b': gain((N_EVEN, hd)),
        'ev_kn_b': gain((N_EVEN, hd)),
        'ev_sink': normal((N_EVEN, B_Q_HEADS), 0.5),
        'ev_w_out': normal((N_EVEN, EVEN_MIX, D_MODEL), EVEN_MIX ** -0.5),
        'od_norm': gain((N_ODD, D_MODEL)),
        'od_w_in': normal((N_ODD, D_MODEL, ODD_IN), D_MODEL ** -0.5),
        'od_qn': gain((N_ODD, hd)),
        'od_kn': gain((N_ODD, hd)),
        'od_lq1': normal((N_ODD, hd), 0.1),
        'od_lk1': normal((N_ODD, hd), 0.1),
        'od_lq2': normal((N_ODD, hd), 0.1),
        'od_lk2': normal((N_ODD, hd), 0.1),
        'od_subln': gain((N_ODD, 2 * hd)),
        'od_w_out': normal((N_ODD, ODD_MIX, D_MODEL), ODD_MIX ** -0.5),
        'ffn_norm': gain((DEPTH, D_MODEL)),
        'rg_w': normal((DEPTH, D_MODEL, N_GROUPS), D_MODEL ** -0.5),
        'rg_b': normal((DEPTH, N_GROUPS), 0.01),
        're_w': normal((DEPTH, N_GROUPS, D_MODEL, EXPERTS_PER_GROUP), D_MODEL ** -0.5),
        're_b': normal((DEPTH, N_GROUPS, EXPERTS_PER_GROUP), 0.01),
        'w_gate': normal((DEPTH, N_EXPERTS, D_MODEL, D_EXPERT), D_MODEL ** -0.5),
        'w_up': normal((DEPTH, N_EXPERTS, D_MODEL, D_EXPERT), D_MODEL ** -0.5),
        'w_down': normal((DEPTH, N_EXPERTS, D_EXPERT, D_MODEL), D_EXPERT ** -0.5),
    }


def reference(x_prompt, x_sample, ev_norm, ev_w_in, ev_qn_a, ev_kn_a, ev_qn_b, ev_kn_b, ev_sink, ev_w_out,
              od_norm, od_w_in, od_qn, od_kn, od_lq1, od_lk1, od_lq2, od_lk2, od_subln, od_w_out,
              ffn_norm, rg_w, rg_b, re_w, re_b, w_gate, w_up, w_down):
    y_prompt = trunk(x_prompt, ev_norm, ev_w_in, ev_qn_a, ev_kn_a, ev_qn_b, ev_kn_b, ev_sink, ev_w_out,
                     od_norm, od_w_in, od_qn, od_kn, od_lq1, od_lk1, od_lq2, od_lk2, od_subln, od_w_out,
                     ffn_norm, rg_w, rg_b, re_w, re_b, w_gate, w_up, w_down)
    y_sample = trunk(x_sample, ev_norm, ev_w_in, ev_qn_a, ev_kn_a, ev_qn_b, ev_kn_b, ev_sink, ev_w_out,
                     od_norm, od_w_in, od_qn, od_kn, od_lq1, od_lk1, od_lq2, od_lk2, od_subln, od_w_out,
                     ffn_norm, rg_w, rg_b, re_w, re_b, w_gate, w_up, w_down)
    return (y_prompt, y_sample)
```

```python
import functools
import math

import jax
import jax.numpy as jnp
from jax import lax
from jax.experimental import pallas as pl
from jax.experimental.pallas import tpu as pltpu

F32 = jnp.float32
BF16 = jnp.bfloat16

D_MODEL = 1024
HEAD_DIM = 64
A_HEADS = 8
DILATION_GROUPS = ((128, 1), (512, 4), (2048, 16))
B_Q_HEADS = 8
B_KV_HEADS = 2
B_HALF_WINDOW = 128
C_HEADS = 8
N_GROUPS = 4
EXPERTS_PER_GROUP = 8
N_EXPERTS = N_GROUPS * EXPERTS_PER_GROUP
TOP_K = 2
D_EXPERT = 512
ROPE_THETA = 10000.0
EPS = 1e-6
NEG_INF = -1e30

LANES = 128
MXU_DIM = 256
VMEM_LIMIT_BYTES = 52 * 1024 * 1024

ROW_TILE = 512
COL_CHUNK = 256
BAND_TQ = 128
BAND_BLOCK_BYTES = 2 * 1024 * 1024
DIFF_TQ = 256
DIFF_TK = 512
EXPERT_TILE = 256
ROUTER_COLS = 128
ROUTER_E0 = 8


def _cparams(sem):
    return pltpu.CompilerParams(dimension_semantics=sem, vmem_limit_bytes=VMEM_LIMIT_BYTES)


def _lane_iota(shape):
    return lax.broadcasted_iota(jnp.int32, shape, len(shape) - 1)


def _inproj_kernel(x_ref, g_ref, w_ref, hg_ref, cos_ref, sin_ref, bd_ref, *out_refs, plan):
    x = x_ref[...]
    ms = jnp.mean(x * x, axis=-1, keepdims=True)
    xn = (x * lax.rsqrt(ms + EPS) * g_ref[...]).astype(BF16)
    lane = _lane_iota((1, LANES))
    first_half = (lane % HEAD_DIM) < (HEAD_DIM // 2)
    cos = cos_ref[...]
    sin = sin_ref[...]
    for oi, c0, width, normed, scale in plan:
        for cc in range(0, width, COL_CHUNK):
            cw = min(COL_CHUNK, width - cc)
            y = jnp.dot(xn, w_ref[:, c0 + cc:c0 + cc + cw], preferred_element_type=F32)
            if normed:
                hs = jnp.dot((y * y).astype(BF16), bd_ref[:cw, :cw], preferred_element_type=F32)
                y = y * lax.rsqrt(hs * (1.0 / HEAD_DIM) + EPS) * hg_ref[:, c0 + cc:c0 + cc + cw]
                for j in range(cw // LANES):
                    yj = y[:, j * LANES:(j + 1) * LANES]
                    rot = jnp.where(first_half, pltpu.roll(yj, LANES - HEAD_DIM // 2, 1),
                                    pltpu.roll(yj, HEAD_DIM // 2, 1))
                    r = yj * cos + rot * sin
                    if scale != 1.0:
                        r = r * scale
                    out_refs[oi][:, cc + j * LANES:cc + (j + 1) * LANES] = r.astype(BF16)
            else:
                out_refs[oi][:, cc:cc + cw] = y.astype(BF16)


def _inproj(x, g, w, hg, cos_t, sin_t, bd, plan, out_widths, seq):
    T, D = x.shape
    N = w.shape[1]
    tm = ROW_TILE
    n_pos_blocks = seq // tm
    const = lambda i: (0, 0)
    return pl.pallas_call(
        functools.partial(_inproj_kernel, plan=plan),
        grid=(T // tm,),
        in_specs=[
            pl.BlockSpec((tm, D), lambda i: (i, 0)),
            pl.BlockSpec((1, D), const),
            pl.BlockSpec((D, N), const),
            pl.BlockSpec((1, N), const),
            pl.BlockSpec((tm, LANES), lambda i: (i % n_pos_blocks, 0)),
            pl.BlockSpec((tm, LANES), lambda i: (i % n_pos_blocks, 0)),
            pl.BlockSpec((MXU_DIM, MXU_DIM), const),
        ],
        out_specs=[pl.BlockSpec((tm, wd), lambda i: (i, 0)) for wd in out_widths],
        out_shape=[jax.ShapeDtypeStruct((T, wd), BF16) for wd in out_widths],
        compiler_params=_cparams(("parallel",)),
        name="inproj",
    )(x, g, w, hg, cos_t, sin_t, bd)


def _band_window(i, tq, hw, L):
    win = tq + 2 * hw
    start = jnp.clip(i * tq - hw, 0, L - win)
    start = pl.multiple_of(start, 64)
    qpos = i * tq + lax.broadcasted_iota(jnp.int32, (tq, win), 0)
    kpos = start + lax.broadcasted_iota(jnp.int32, (tq, win), 1)
    valid = jnp.abs(kpos - qpos) <= hw
    return start, win, valid


def _band_a_kernel(q_ref, k_ref, v_ref, o_ref, lse_ref, *, hw, tq, L, wb):
    lane = _lane_iota((1, LANES))
    head0 = lane < HEAD_DIM

    def qblock(i, carry):
        start, win, valid = _band_window(i, tq, hw, L)
        rows = pl.ds(pl.multiple_of(i * tq, tq), tq)
        for sl in range(wb // LANES):
            cols = slice(sl * LANES, (sl + 1) * LANES)
            q = q_ref[0, rows, cols]
            kw = k_ref[0, pl.ds(start, win), cols]
            vw = v_ref[0, pl.ds(start, win), cols]
            outs = []
            lses = []
            for hmask in (head0, jnp.logical_not(head0)):
                qh = jnp.where(hmask, q, jnp.zeros_like(q))
                s = lax.dot_general(qh, kw, (((1,), (1,)), ((), ())), preferred_element_type=F32)
                s = jnp.where(valid, s, NEG_INF)
                m = jnp.max(s, axis=-1, keepdims=True)
                p = jnp.exp(s - m)
                den = jnp.sum(p, axis=-1, keepdims=True)
                pv = jnp.dot(p.astype(BF16), vw, preferred_element_type=F32)
                outs.append(pv / den)
                lses.append(m + jnp.log(den))
            o_ref[0, rows, cols] = jnp.where(head0, outs[0], outs[1]).astype(o_ref.dtype)
            lse_ref[0, rows, cols] = jnp.where(head0, lses[0], lses[1])
        return carry

    lax.fori_loop(0, L // tq, qblock, 0)


def _band_a(q, k, v, hw):
    NB, L, C = q.shape
    tq = BAND_TQ
    assert L >= tq + 2 * hw and L % tq == 0
    wb = min(C, A_HEADS * HEAD_DIM)
    while L * wb * 2 > BAND_BLOCK_BYTES and wb > LANES:
        wb //= 2
    spec = pl.BlockSpec((1, L, wb), lambda n, c: (n, 0, c))
    return pl.pallas_call(
        functools.partial(_band_a_kernel, hw=hw, tq=tq, L=L, wb=wb),
        grid=(NB, C // wb),
        in_specs=[spec, spec, spec],
        out_specs=[spec, spec],
        out_shape=[jax.ShapeDtypeStruct((NB, L, C), BF16), jax.ShapeDtypeStruct((NB, L, C), F32)],
        compiler_params=_cparams(("parallel", "parallel")),
        name="band_a",
    )(q, k, v)


def _band_b_kernel(sink_ref, q_ref, k_ref, v_ref, o_ref, *, hw, tq, L):
    lane = _lane_iota((1, LANES))
    head0 = lane < HEAD_DIM
    hk = pl.program_id(1)
    group = B_Q_HEADS // B_KV_HEADS
    sink = jnp.concatenate(
        [jnp.full((tq, 1), sink_ref[hk * group + g], F32) for g in range(group)], axis=0)

    def qblock(i, carry):
        start, win, valid = _band_window(i, tq, hw, L)
        rows = pl.ds(pl.multiple_of(i * tq, tq), tq)
        kw = k_ref[0, pl.ds(start, win), :]
        vw = v_ref[0, pl.ds(start, win), :]
        qs = []
        for g in range(group):
            q = q_ref[0, rows, (g // 2) * LANES:(g // 2 + 1) * LANES]
            hmask = head0 if g % 2 == 0 else jnp.logical_not(head0)
            qs.append(jnp.where(hmask, q, jnp.zeros_like(q)))
        qq = jnp.concatenate(qs, axis=0)
        s = lax.dot_general(qq, kw, (((1,), (1,)), ((), ())), preferred_element_type=F32)
        valid4 = jnp.concatenate([valid] * group, axis=0)
        s = jnp.where(valid4, s, NEG_INF)
        m = jnp.maximum(jnp.max(s, axis=-1, keepdims=True), sink)
        p = jnp.exp(s - m)
        den = jnp.sum(p, axis=-1, keepdims=True) + jnp.exp(sink - m)
        pv = jnp.dot(p.astype(BF16), vw, preferred_element_type=F32) / den
        for pair in range(group // 2):
            a = pv[(2 * pair) * tq:(2 * pair + 1) * tq]
            b = pv[(2 * pair + 1) * tq:(2 * pair + 2) * tq]
            o_ref[0, rows, pair * LANES:(pair + 1) * LANES] = jnp.where(head0, a, b).astype(o_ref.dtype)
        return carry

    lax.fori_loop(0, L // tq, qblock, 0)


def _band_b(sink, q, kdup, vdup):
    B, S, C = q.shape
    tq = BAND_TQ
    hw = B_HALF_WINDOW
    qw = C // B_KV_HEADS
    assert S >= tq + 2 * hw
    return pl.pallas_call(
        functools.partial(_band_b_kernel, hw=hw, tq=tq, L=S),
        grid=(B, B_KV_HEADS),
        in_specs=[
            pl.BlockSpec(memory_space=pltpu.SMEM),
            pl.BlockSpec((1, S, qw), lambda b, h: (b, 0, h)),
            pl.BlockSpec((1, S, LANES), lambda b, h: (b, 0, h)),
            pl.BlockSpec((1, S, LANES), lambda b, h: (b, 0, h)),
        ],
        out_specs=pl.BlockSpec((1, S, qw), lambda b, h: (b, 0, h)),
        out_shape=jax.ShapeDtypeStruct((B, S, C), BF16),
        compiler_params=_cparams(("parallel", "parallel")),
        name="band_b",
    )(sink, q, kdup, vdup)


def _outproj_even_kernel(o1_ref, o4_ref, o16_ref, l1_ref, l4_ref, l16_ref, ob_ref, w_ref, x_ref, y_ref):
    l1, l4, l16 = l1_ref[...], l4_ref[...], l16_ref[...]
    m = jnp.maximum(jnp.maximum(l1, l4), l16)
    e1, e4, e16 = jnp.exp(l1 - m), jnp.exp(l4 - m), jnp.exp(l16 - m)
    num = (e1 * o1_ref[...].astype(F32) + e4 * o4_ref[...].astype(F32) + e16 * o16_ref[...].astype(F32))
    oa = (num / (e1 + e4 + e16)).astype(BF16)
    ka = oa.shape[1]
    acc = jnp.dot(oa, w_ref[:ka, :], preferred_element_type=F32)
    acc = acc + jnp.dot(ob_ref[...], w_ref[ka:, :], preferred_element_type=F32)
    y_ref[...] = x_ref[...] + acc


def _outproj_even(o1, o4, o16, l1, l4, l16, ob, w, x):
    T, D = x.shape
    tm = ROW_TILE
    half = o1.shape[1]
    row = lambda wd: pl.BlockSpec((tm, wd), lambda i: (i, 0))
    return pl.pallas_call(
        _outproj_even_kernel,
        grid=(T // tm,),
        in_specs=[row(half)] * 7 + [pl.BlockSpec(w.shape, lambda i: (0, 0)), row(D)],
        out_specs=row(D),
        out_shape=jax.ShapeDtypeStruct((T, D), F32),
        compiler_params=_cparams(("parallel",)),
        name="outproj_even",
    )(o1, o4, o16, l1, l4, l16, ob, w, x)


def _outproj_kernel(o_ref, w_ref, x_ref, y_ref):
    y_ref[...] = x_ref[...] + jnp.dot(o_ref[...], w_ref[...], preferred_element_type=F32)


def _outproj(o, w, x):
    T, D = x.shape
    tm = ROW_TILE
    row = lambda wd: pl.BlockSpec((tm, wd), lambda i: (i, 0))
    return pl.pallas_call(
        _outproj_kernel,
        grid=(T // tm,),
        in_specs=[row(o.shape[1]), pl.BlockSpec(w.shape, lambda i: (0, 0)), row(D)],
        out_specs=row(D),
        out_shape=jax.ShapeDtypeStruct((T, D), F32),
        compiler_params=_cparams(("parallel",)),
        name="outproj",
    )(o, w, x)


def _diff_kernel(lam_ref, q_ref, k_ref, v_ref, sub_ref, o_ref, *, tq, tk, S, lam_init):
    lane = _lane_iota((1, LANES))
    comp0 = lane < HEAD_DIM
    lamv = lam_ref[...]
    lam = (jnp.exp(jnp.sum(lamv[0:1] * lamv[1:2], axis=-1, keepdims=True))
           - jnp.exp(jnp.sum(lamv[2:3] * lamv[3:4], axis=-1, keepdims=True)) + lam_init)
    q = q_ref[0]
    qq = jnp.concatenate([jnp.where(comp0, q, jnp.zeros_like(q)),
                          jnp.where(comp0, jnp.zeros_like(q), q)], axis=0)

    def kv_step(j, carry):
        m, l, acc = carry
        rows = pl.ds(pl.multiple_of(j * tk, tk), tk)
        kb = k_ref[0, rows, :]
        vb = v_ref[0, rows, :]
        s = lax.dot_general(qq, kb, (((1,), (1,)), ((), ())), preferred_element_type=F32)
        m_new = jnp.maximum(m, jnp.max(s, axis=-1, keepdims=True))
        alpha = jnp.exp(m - m_new)
        p = jnp.exp(s - m_new)
        l = alpha * l + jnp.sum(p, axis=-1, keepdims=True)
        acc = alpha * acc + jnp.dot(p.astype(BF16), vb, preferred_element_type=F32)
        return m_new, l, acc

    init = (jnp.full((2 * tq, 1), NEG_INF, F32), jnp.zeros((2 * tq, 1), F32),
            jnp.zeros((2 * tq, LANES), F32))
    _, l, acc = lax.fori_loop(0, S // tk, kv_step, init)
    a = acc / l
    o = a[:tq] - lam * a[tq:]
    ms = jnp.mean(o * o, axis=-1, keepdims=True)
    o = o * lax.rsqrt(ms + EPS) * sub_ref[...] * (1.0 - lam_init)
    o_ref[0] = o.astype(o_ref.dtype)


def _diff_attention(lam_vecs, q, k, v, subln, lam_init):
    B, S, C = q.shape
    tq, tk = DIFF_TQ, DIFF_TK
    assert S % tq == 0 and S % tk == 0
    return pl.pallas_call(
        functools.partial(_diff_kernel, tq=tq, tk=tk, S=S, lam_init=lam_init),
        grid=(B, C // LANES, S // tq),
        in_specs=[
            pl.BlockSpec(lam_vecs.shape, lambda b, h, i: (0, 0)),
            pl.BlockSpec((1, tq, LANES), lambda b, h, i: (b, i, h)),
            pl.BlockSpec((1, S, LANES), lambda b, h, i: (b, 0, h)),
            pl.BlockSpec((1, S, LANES), lambda b, h, i: (b, 0, h)),
            pl.BlockSpec((1, LANES), lambda b, h, i: (0, 0)),
        ],
        out_specs=pl.BlockSpec((1, tq, LANES), lambda b, h, i: (b, i, h)),
        out_shape=jax.ShapeDtypeStruct((B, S, C), BF16),
        compiler_params=_cparams(("parallel", "parallel", "arbitrary")),
        name="diff_attn",
    )(lam_vecs, q, k, v, subln)


def _router_kernel(x_ref, g_ref, whi_ref, wlo_ref, b_ref, xn_ref, gate_ref, eid_ref):
    x = x_ref[...]
    ms = jnp.mean(x * x, axis=-1, keepdims=True)
    xn = x * lax.rsqrt(ms + EPS) * g_ref[...]
    xhi = xn.astype(BF16)
    xn_ref[...] = xhi
    xlo = (xn - xhi.astype(F32)).astype(BF16)
    whi = whi_ref[...]
    logits = (jnp.dot(xhi, whi, preferred_element_type=F32)
              + jnp.dot(xlo, whi, preferred_element_type=F32)
              + jnp.dot(xhi, wlo_ref[...], preferred_element_type=F32)) + b_ref[...]
    lane = _lane_iota(logits.shape).astype(F32)
    big = float(4 * LANES)
    lg = jnp.where(lane < N_GROUPS, logits, NEG_INF)
    mg = jnp.max(lg, axis=-1, keepdims=True)
    g_sel = jnp.min(jnp.where(lg == mg, lane, big), axis=-1, keepdims=True)
    p_sel = 1.0 / jnp.sum(jnp.exp(lg - mg), axis=-1, keepdims=True)
    e_lo = ROUTER_E0 + g_sel * EXPERTS_PER_GROUP
    le = jnp.where((lane >= e_lo) & (lane < e_lo + EXPERTS_PER_GROUP), logits, NEG_INF)
    v1 = jnp.max(le, axis=-1, keepdims=True)
    i1 = jnp.min(jnp.where(le == v1, lane, big), axis=-1, keepdims=True)
    le2 = jnp.where(lane == i1, NEG_INF, le)
    v2 = jnp.max(le2, axis=-1, keepdims=True)
    i2 = jnp.min(jnp.where(le2 == v2, lane, big), axis=-1, keepdims=True)
    e2 = jnp.exp(v2 - v1)
    den = 1.0 + e2
    g1 = (1.0 / den) * p_sel
    g2 = (e2 / den) * p_sel
    gate_ref[...] = jnp.where(lane == 0, g1, jnp.where(lane == 1, g2, 0.0))
    eid = jnp.where(lane == 0, i1 - ROUTER_E0, jnp.where(lane == 1, i2 - ROUTER_E0, 0.0))
    eid_ref[...] = eid.astype(jnp.int32)


def _router(x, g, whi, wlo, bias):
    T, D = x.shape
    tm = ROW_TILE
    const = lambda i: (0, 0)
    row = lambda wd: pl.BlockSpec((tm, wd), lambda i: (i, 0))
    return pl.pallas_call(
        _router_kernel,
        grid=(T // tm,),
        in_specs=[row(D), pl.BlockSpec((1, D), const), pl.BlockSpec((D, ROUTER_COLS), const),
                  pl.BlockSpec((D, ROUTER_COLS), const), pl.BlockSpec((1, ROUTER_COLS), const)],
        out_specs=[row(D), row(ROUTER_COLS), row(ROUTER_COLS)],
        out_shape=[jax.ShapeDtypeStruct((T, D), BF16), jax.ShapeDtypeStruct((T, ROUTER_COLS), F32),
                   jax.ShapeDtypeStruct((T, ROUTER_COLS), jnp.int32)],
        compiler_params=_cparams(("parallel",)),
        name="router",
    )(x, g, whi, wlo, bias)


def _expert_kernel(te_ref, nt_ref, xs_ref, gate_ref, wg_ref, wu_ref, wd_ref, y_ref):
    t = pl.program_id(0)

    @pl.when(t < nt_ref[0])
    def _():
        xs = xs_ref[...]
        a = jnp.dot(xs, wg_ref[0], preferred_element_type=F32)
        u = jnp.dot(xs, wu_ref[0], preferred_element_type=F32)
        h = (a * jax.nn.sigmoid(a) * u).astype(BF16)
        y = jnp.dot(h, wd_ref[0], preferred_element_type=F32)
        y_ref[...] = y * gate_ref[...]

    @pl.when(t >= nt_ref[0])
    def _():
        y_ref[...] = jnp.zeros_like(y_ref)


def _experts(tile_expert, n_tiles, xs, gate_col, wg, wu, wd):
    Mp, D = xs.shape
    te = EXPERT_TILE
    grid_spec = pltpu.PrefetchScalarGridSpec(
        num_scalar_prefetch=2,
        grid=(Mp // te,),
        in_specs=[
            pl.BlockSpec((te, D), lambda t, te_r, nt_r: (t, 0)),
            pl.BlockSpec((te, 1), lambda t, te_r, nt_r: (t, 0)),
            pl.BlockSpec((1, D, D_EXPERT), lambda t, te_r, nt_r: (te_r[t], 0, 0)),
            pl.BlockSpec((1, D, D_EXPERT), lambda t, te_r, nt_r: (te_r[t], 0, 0)),
            pl.BlockSpec((1, D_EXPERT, D), lambda t, te_r, nt_r: (te_r[t], 0, 0)),
        ],
        out_specs=pl.BlockSpec((te, D), lambda t, te_r, nt_r: (t, 0)),
    )
    return pl.pallas_call(
        _expert_kernel,
        grid_spec=grid_spec,
        out_shape=jax.ShapeDtypeStruct((Mp, D), F32),
        compiler_params=_cparams(("arbitrary",)),
        name="experts",
    )(tile_expert, n_tiles, xs, gate_col, wg, wu, wd)


def _moe(x, p):
    T, D = x.shape
    xn, gate, eid = _router(x, p["ffn_norm"], p["r_whi"], p["r_wlo"], p["r_bias"])
    M = T * TOP_K
    te = EXPERT_TILE
    Mp = M + N_EXPERTS * te
    flat_e = eid[:, :TOP_K].reshape(M)
    flat_g = gate[:, :TOP_K].reshape(M)
    onehot = (flat_e[:, None] == jnp.arange(N_EXPERTS, dtype=jnp.int32)[None, :]).astype(jnp.int32)
    csum = jnp.cumsum(onehot, axis=0)
    rank = jnp.sum(onehot * csum, axis=1) - 1
    counts = csum[-1]
    padded = ((counts + te - 1) // te) * te
    pend = jnp.cumsum(padded)
    pstart = pend - padded
    slot = pstart[flat_e] + rank
    tok_of_slot = jnp.zeros((Mp,), jnp.int32).at[slot].set(jnp.arange(M, dtype=jnp.int32) // TOP_K)
    gate_of_slot = jnp.zeros((Mp,), F32).at[slot].set(flat_g)
    tile_ids = jnp.arange(Mp // te, dtype=jnp.int32)
    tile_expert = jnp.minimum(jnp.searchsorted(pend // te, tile_ids, side="right"),
                              N_EXPERTS - 1).astype(jnp.int32)
    n_tiles = (pend[-1] // te).astype(jnp.int32).reshape(1)
    xs = jnp.take(xn, tok_of_slot, axis=0)
    ys = _experts(tile_expert, n_tiles, xs, gate_of_slot[:, None], p["w_gate"], p["w_up"], p["w_down"])
    slot2 = slot.reshape(T, TOP_K)
    return x + jnp.take(ys, slot2[:, 0], axis=0) + jnp.take(ys, slot2[:, 1], axis=0)


def _rope_tables(seq):
    half = HEAD_DIM // 2
    inv = ROPE_THETA ** (-jnp.arange(half, dtype=F32) / half)
    ang = jnp.arange(seq, dtype=F32)[:, None] * inv[None, :]
    cos = jnp.tile(jnp.cos(ang), (1, LANES // half))
    sin = jnp.sin(ang)
    sin = jnp.tile(jnp.concatenate([-sin, sin], axis=1), (1, LANES // HEAD_DIM))
    return cos, sin


def _head_gain(g, heads):
    return jnp.tile(g.astype(F32), (heads,))


def _prepare(ev_norm, ev_w_in, ev_qn_a, ev_kn_a, ev_qn_b, ev_kn_b, ev_sink, ev_w_out,
             od_norm, od_w_in, od_qn, od_kn, od_lq1, od_lk1, od_lq2, od_lk2, od_subln, od_w_out,
             ffn_norm, rg_w, rg_b, re_w, re_b, w_gate, w_up, w_down):
    hd = HEAD_DIM
    P = {}
    r = jnp.arange(MXU_DIM) // hd
    P["bd"] = (r[:, None] == r[None, :]).astype(BF16)
    w = ev_w_in[0]
    a3 = 3 * A_HEADS * hd
    qb_end = a3 + B_Q_HEADS * hd
    kb = [w[:, qb_end + h * hd:qb_end + (h + 1) * hd] for h in range(B_KV_HEADS)]
    vb0 = qb_end + B_KV_HEADS * hd
    vb = [w[:, vb0 + h * hd:vb0 + (h + 1) * hd] for h in range(B_KV_HEADS)]
    P["ev_w_in"] = jnp.concatenate([w[:, :qb_end]] + [c for h in kb for c in (h, h)]
                                   + [c for h in vb for c in (h, h)], axis=1).astype(BF16)
    ones = jnp.ones((A_HEADS * hd,), F32)
    P["ev_hg"] = jnp.concatenate([
        _head_gain(ev_qn_a[0], A_HEADS), _head_gain(ev_kn_a[0], A_HEADS), ones,
        _head_gain(ev_qn_b[0], B_Q_HEADS), _head_gain(ev_kn_b[0], 2 * B_KV_HEADS),
        jnp.ones((2 * B_KV_HEADS * hd,), F32)])[None, :]
    P["ev_norm"] = ev_norm[0][None, :]
    P["ev_sink"] = ev_sink[0].astype(F32)
    P["ev_w_out"] = ev_w_out[0].astype(BF16)
    P["od_w_in"] = od_w_in[0].astype(BF16)
    P["od_hg"] = jnp.concatenate([_head_gain(od_qn[0], 2 * C_HEADS), _head_gain(od_kn[0], 2 * C_HEADS),
                                  jnp.ones((2 * C_HEADS * hd,), F32)])[None, :]
    P["od_norm"] = od_norm[0][None, :]
    P["od_lam"] = jnp.stack([od_lq1[0], od_lk1[0], od_lq2[0], od_lk2[0]]).astype(F32)
    P["od_subln"] = od_subln[0][None, :].astype(F32)
    P["od_w_out"] = od_w_out[0].astype(BF16)
    P["moe"] = []
    for layer in range(ffn_norm.shape[0]):
        wr = jnp.zeros((D_MODEL, ROUTER_COLS), F32)
        wr = wr.at[:, :N_GROUPS].set(rg_w[layer])
        we = jnp.transpose(re_w[layer], (1, 0, 2)).reshape(D_MODEL, N_EXPERTS)
        wr = wr.at[:, ROUTER_E0:ROUTER_E0 + N_EXPERTS].set(we)
        bias = jnp.zeros((ROUTER_COLS,), F32).at[:N_GROUPS].set(rg_b[layer])
        bias = bias.at[ROUTER_E0:ROUTER_E0 + N_EXPERTS].set(re_b[layer].reshape(N_EXPERTS))
        whi = wr.astype(BF16)
        P["moe"].append({
            "ffn_norm": ffn_norm[layer][None, :],
            "r_whi": whi, "r_wlo": (wr - whi.astype(F32)).astype(BF16), "r_bias": bias[None, :],
            "w_gate": w_gate[layer].astype(BF16), "w_up": w_up[layer].astype(BF16),
            "w_down": w_down[layer].astype(BF16),
        })
    return P


def _even_mixer(x, P, B, S):
    T = B * S
    hd = HEAD_DIM
    aw = A_HEADS * hd
    cos_t, sin_t = _rope_tables(S)
    scale = hd ** -0.5
    plan = ((0, 0, aw, True, scale), (1, aw, aw, True, 1.0), (2, 2 * aw, aw, False, 1.0),
            (3, 3 * aw, aw, True, scale), (4, 4 * aw, 2 * LANES, True, 1.0),
            (5, 4 * aw + 2 * LANES, 2 * LANES, False, 1.0))
    qa, ka, va, qb, kb, vb = _inproj(x, P["ev_norm"], P["ev_w_in"], P["ev_hg"], cos_t, sin_t, P["bd"],
                                     plan, (aw, aw, aw, aw, 2 * LANES, 2 * LANES), S)
    outs, lses = [], []
    for window, dil in DILATION_GROUPS:
        L = S // dil
        view = lambda t: t.reshape(B, L, dil * aw)
        o, lse = _band_a(view(qa), view(ka), view(va), window // (2 * dil))
        outs.append(o.reshape(T, aw))
        lses.append(lse.reshape(T, aw))
    ob = _band_b(P["ev_sink"], qb.reshape(B, S, aw), kb.reshape(B, S, 2 * LANES), vb.reshape(B, S, 2 * LANES))
    return _outproj_even(outs[0], outs[1], outs[2], lses[0], lses[1], lses[2], ob.reshape(T, aw),
                         P["ev_w_out"], x)


def _diff_mixer(x, P, B, S, lam_init):
    T = B * S
    hd = HEAD_DIM
    cw = 2 * C_HEADS * hd
    cos_t, sin_t = _rope_tables(S)
    plan = ((0, 0, cw, True, hd ** -0.5), (1, cw, cw, True, 1.0), (2, 2 * cw, cw, False, 1.0))
    q, k, v = _inproj(x, P["od_norm"], P["od_w_in"], P["od_hg"], cos_t, sin_t, P["bd"], plan, (cw, cw, cw), S)
    o = _diff_attention(P["od_lam"], q.reshape(B, S, cw), k.reshape(B, S, cw), v.reshape(B, S, cw),
                        P["od_subln"], lam_init)
    return _outproj(o.reshape(T, cw), P["od_w_out"], x)


def _trunk(x3, P):
    B, S, D = x3.shape
    x = x3.reshape(B * S, D)
    x = _even_mixer(x, P, B, S)
    x = _moe(x, P["moe"][0])
    lam_init = 0.8 - 0.6 * math.exp(-0.3 * 1)
    x = _diff_mixer(x, P, B, S, lam_init)
    x = _moe(x, P["moe"][1])
    return x.reshape(B, S, D)


def kernel(x_prompt, x_sample, ev_norm, ev_w_in, ev_qn_a, ev_kn_a, ev_qn_b, ev_kn_b, ev_sink, ev_w_out,
           od_norm, od_w_in, od_qn, od_kn, od_lq1, od_lk1, od_lq2, od_lk2, od_subln, od_w_out,
           ffn_norm, rg_w, rg_b, re_w, re_b, w_gate, w_up, w_down):
    P = _prepare(ev_norm, ev_w_in, ev_qn_a, ev_kn_a, ev_qn_b, ev_kn_b, ev_sink, ev_w_out,
                 od_norm, od_w_in, od_qn, od_kn, od_lq1, od_lk1, od_lq2, od_lk2, od_subln, od_w_out,
                 ffn_norm, rg_w, rg_b, re_w, re_b, w_gate, w_up, w_down)
    return (_trunk(x_prompt, P), _trunk(x_sample, P))
```

```python
import functools
import math

import jax
import jax.numpy as jnp
from jax import lax
from jax.experimental import pallas as pl
from jax.experimental.pallas import tpu as pltpu

F32 = jnp.float32
BF16 = jnp.bfloat16

D_MODEL = 1024
HEAD_DIM = 64
A_HEADS = 8
DILATION_GROUPS = ((128, 1), (512, 4), (2048, 16))
B_Q_HEADS = 8
B_KV_HEADS = 2
B_HALF_WINDOW = 128
C_HEADS = 8
N_GROUPS = 4
EXPERTS_PER_GROUP = 8
N_EXPERTS = N_GROUPS * EXPERTS_PER_GROUP
TOP_K = 2
D_EXPERT = 512
ROPE_THETA = 10000.0
EPS = 1e-6
NEG_INF = -1e30

LANES = 128
MXU_DIM = 256
VMEM_LIMIT_BYTES = 52 * 1024 * 1024

ROW_TILE = 512
COL_CHUNK = 256
BAND_TQ = 128
BAND_BLOCK_BYTES = 2 * 1024 * 1024
DIFF_TQ = 512
DIFF_TK = 256
DIFF_UNROLL = 8
DIFF_ONES_ROWS = 16
LOG2_E = 1.4426950408889634
EXPERT_TILE = 256
ROUTER_COLS = 128
ROUTER_E0 = 8


def _cparams(sem):
    return pltpu.CompilerParams(dimension_semantics=sem, vmem_limit_bytes=VMEM_LIMIT_BYTES)


def _lane_iota(shape):
    return lax.broadcasted_iota(jnp.int32, shape, len(shape) - 1)


def _inproj_kernel(x_ref, g_ref, w_ref, hg_ref, cos_ref, sin_ref, bd_ref, *refs, plan, n_out):
    out_refs = refs[:n_out]
    stage_ref = refs[n_out] if len(refs) > n_out else None
    x = x_ref[...]
    tm = x.shape[0]
    ms = jnp.mean(x * x, axis=-1, keepdims=True)
    xn = (x * lax.rsqrt(ms + EPS) * g_ref[...]).astype(BF16)
    lane = _lane_iota((1, LANES))
    first_half = (lane % HEAD_DIM) < (HEAD_DIM // 2)
    cos = cos_ref[...]
    sin = sin_ref[...]
    for oi, c0, width, normed, scale, views in plan:
        for cc in range(0, width, COL_CHUNK):
            cw = min(COL_CHUNK, width - cc)
            y = jnp.dot(xn, w_ref[:, c0 + cc:c0 + cc + cw], preferred_element_type=F32)
            if normed:
                hs = jnp.dot((y * y).astype(BF16), bd_ref[:cw, :cw], preferred_element_type=F32)
                y = y * lax.rsqrt(hs * (1.0 / HEAD_DIM) + EPS) * hg_ref[:, c0 + cc:c0 + cc + cw]
            for j in range(cw // LANES):
                r = y[:, j * LANES:(j + 1) * LANES]
                if normed:
                    rot = jnp.where(first_half, pltpu.roll(r, LANES - HEAD_DIM // 2, 1),
                                    pltpu.roll(r, HEAD_DIM // 2, 1))
                    r = r * cos + rot * sin
                    if scale != 1.0:
                        r = r * scale
                cols = slice(cc + j * LANES, cc + (j + 1) * LANES)
                out_refs[oi][:, cols] = r.astype(BF16)
                if views:
                    stage_ref[cc // LANES + j] = r
        for d, vi in views:
            for rr in range(d):
                for j in range(width // LANES):
                    out_refs[vi][:, rr * width + j * LANES:rr * width + (j + 1) * LANES] = (
                        stage_ref[j, pl.ds(rr, tm // d, stride=d), :].astype(BF16))


def _inproj(x, g, w, hg, cos_t, sin_t, bd, plan, out_blocks, seq, stage_width=0):
    T, D = x.shape
    N = w.shape[1]
    tm = ROW_TILE
    n_pos_blocks = seq // tm
    const = lambda i: (0, 0)
    scratch = [pltpu.VMEM((stage_width // LANES, tm, LANES), F32)] if stage_width else []
    return pl.pallas_call(
        functools.partial(_inproj_kernel, plan=plan, n_out=len(out_blocks)),
        grid=(T // tm,),
        in_specs=[
            pl.BlockSpec((tm, D), lambda i: (i, 0)),
            pl.BlockSpec((1, D), const),
            pl.BlockSpec((D, N), const),
            pl.BlockSpec((1, N), const),
            pl.BlockSpec((tm, LANES), lambda i: (i % n_pos_blocks, 0)),
            pl.BlockSpec((tm, LANES), lambda i: (i % n_pos_blocks, 0)),
            pl.BlockSpec((MXU_DIM, MXU_DIM), const),
        ],
        out_specs=[pl.BlockSpec((br, bc), lambda i: (i, 0)) for br, bc in out_blocks],
        out_shape=[jax.ShapeDtypeStruct((T // tm * br, bc), BF16) for br, bc in out_blocks],
        scratch_shapes=scratch,
        compiler_params=_cparams(("parallel",)),
        name="inproj",
    )(x, g, w, hg, cos_t, sin_t, bd)


def _band_window(i, tq, hw, L):
    win = tq + 2 * hw
    start = jnp.clip(i * tq - hw, 0, L - win)
    start = pl.multiple_of(start, 64)
    qpos = i * tq + lax.broadcasted_iota(jnp.int32, (tq, win), 0)
    kpos = start + lax.broadcasted_iota(jnp.int32, (tq, win), 1)
    valid = jnp.abs(kpos - qpos) <= hw
    return start, win, valid


def _band_a_kernel(q_ref, k_ref, v_ref, o_ref, lse_ref, *, hw, tq, L, wb):
    lane = _lane_iota((1, LANES))
    head0 = lane < HEAD_DIM

    def qblock(i, carry):
        start, win, valid = _band_window(i, tq, hw, L)
        rows = pl.ds(pl.multiple_of(i * tq, tq), tq)
        for sl in range(wb // LANES):
            cols = slice(sl * LANES, (sl + 1) * LANES)
            q = q_ref[0, rows, cols]
            kw = k_ref[0, pl.ds(start, win), cols]
            vw = v_ref[0, pl.ds(start, win), cols]
            outs = []
            lses = []
            for hmask in (head0, jnp.logical_not(head0)):
                qh = jnp.where(hmask, q, jnp.zeros_like(q))
                s = lax.dot_general(qh, kw, (((1,), (1,)), ((), ())), preferred_element_type=F32)
                s = jnp.where(valid, s, NEG_INF)
                m = jnp.max(s, axis=-1, keepdims=True)
                p = jnp.exp(s - m)
                den = jnp.sum(p, axis=-1, keepdims=True)
                pv = jnp.dot(p.astype(BF16), vw, preferred_element_type=F32)
                outs.append(pv / den)
                lses.append(m + jnp.log(den))
            o_ref[0, rows, cols] = jnp.where(head0, outs[0], outs[1]).astype(o_ref.dtype)
            lse_ref[0, rows, cols] = jnp.where(head0, lses[0], lses[1])
        return carry

    lax.fori_loop(0, L // tq, qblock, 0)


def _band_a(q, k, v, hw):
    NB, L, C = q.shape
    tq = BAND_TQ
    assert L >= tq + 2 * hw and L % tq == 0
    wb = min(C, A_HEADS * HEAD_DIM)
    while L * wb * 2 > BAND_BLOCK_BYTES and wb > LANES:
        wb //= 2
    spec = pl.BlockSpec((1, L, wb), lambda n, c: (n, 0, c))
    return pl.pallas_call(
        functools.partial(_band_a_kernel, hw=hw, tq=tq, L=L, wb=wb),
        grid=(NB, C // wb),
        in_specs=[spec, spec, spec],
        out_specs=[spec, spec],
        out_shape=[jax.ShapeDtypeStruct((NB, L, C), BF16), jax.ShapeDtypeStruct((NB, L, C), F32)],
        compiler_params=_cparams(("parallel", "parallel")),
        name="band_a",
    )(q, k, v)


def _band_b_kernel(sink_ref, q_ref, k_ref, v_ref, o_ref, *, hw, tq, L):
    lane = _lane_iota((1, LANES))
    head0 = lane < HEAD_DIM
    hk = pl.program_id(1)
    group = B_Q_HEADS // B_KV_HEADS
    sink = jnp.concatenate(
        [jnp.full((tq, 1), sink_ref[hk * group + g], F32) for g in range(group)], axis=0)

    def qblock(i, carry):
        start, win, valid = _band_window(i, tq, hw, L)
        rows = pl.ds(pl.multiple_of(i * tq, tq), tq)
        kw = k_ref[0, pl.ds(start, win), :]
        vw = v_ref[0, pl.ds(start, win), :]
        qs = []
        for g in range(group):
            q = q_ref[0, rows, (g // 2) * LANES:(g // 2 + 1) * LANES]
            hmask = head0 if g % 2 == 0 else jnp.logical_not(head0)
            qs.append(jnp.where(hmask, q, jnp.zeros_like(q)))
        qq = jnp.concatenate(qs, axis=0)
        s = lax.dot_general(qq, kw, (((1,), (1,)), ((), ())), preferred_element_type=F32)
        valid4 = jnp.concatenate([valid] * group, axis=0)
        s = jnp.where(valid4, s, NEG_INF)
        m = jnp.maximum(jnp.max(s, axis=-1, keepdims=True), sink)
        p = jnp.exp(s - m)
        den = jnp.sum(p, axis=-1, keepdims=True) + jnp.exp(sink - m)
        pv = jnp.dot(p.astype(BF16), vw, preferred_element_type=F32) / den
        for pair in range(group // 2):
            a = pv[(2 * pair) * tq:(2 * pair + 1) * tq]
            b = pv[(2 * pair + 1) * tq:(2 * pair + 2) * tq]
            o_ref[0, rows, pair * LANES:(pair + 1) * LANES] = jnp.where(head0, a, b).astype(o_ref.dtype)
        return carry

    lax.fori_loop(0, L // tq, qblock, 0)


def _band_b(sink, q, kdup, vdup):
    B, S, C = q.shape
    tq = BAND_TQ
    hw = B_HALF_WINDOW
    qw = C // B_KV_HEADS
    assert S >= tq + 2 * hw
    return pl.pallas_call(
        functools.partial(_band_b_kernel, hw=hw, tq=tq, L=S),
        grid=(B, B_KV_HEADS),
        in_specs=[
            pl.BlockSpec(memory_space=pltpu.SMEM),
            pl.BlockSpec((1, S, qw), lambda b, h: (b, 0, h)),
            pl.BlockSpec((1, S, LANES), lambda b, h: (b, 0, h)),
            pl.BlockSpec((1, S, LANES), lambda b, h: (b, 0, h)),
        ],
        out_specs=pl.BlockSpec((1, S, qw), lambda b, h: (b, 0, h)),
        out_shape=jax.ShapeDtypeStruct((B, S, C), BF16),
        compiler_params=_cparams(("parallel", "parallel")),
        name="band_b",
    )(sink, q, kdup, vdup)


def _outproj_even_kernel(o1_ref, o4_ref, o16_ref, l1_ref, l4_ref, l16_ref, ob_ref, w_ref, x_ref, y_ref,
                         so4, sl4, so16, sl16):
    tm, half = o1_ref.shape

    def token_major(src_ref, stage_ref, d):
        for rr in range(d):
            for j in range(half // LANES):
                stage_ref[j, pl.ds(rr, tm // d, stride=d), :] = (
                    src_ref[:, rr * half + j * LANES:rr * half + (j + 1) * LANES].astype(F32))
        return jnp.concatenate([stage_ref[j] for j in range(half // LANES)], axis=1)

    dils = [d for _, d in DILATION_GROUPS]
    l1 = l1_ref[...]
    o4, l4 = token_major(o4_ref, so4, dils[1]), token_major(l4_ref, sl4, dils[1])
    o16, l16 = token_major(o16_ref, so16, dils[2]), token_major(l16_ref, sl16, dils[2])
    m = jnp.maximum(jnp.maximum(l1, l4), l16)
    e1, e4, e16 = jnp.exp(l1 - m), jnp.exp(l4 - m), jnp.exp(l16 - m)
    num = e1 * o1_ref[...].astype(F32) + e4 * o4 + e16 * o16
    oa = (num / (e1 + e4 + e16)).astype(BF16)
    ka = oa.shape[1]
    acc = jnp.dot(oa, w_ref[:ka, :], preferred_element_type=F32)
    acc = acc + jnp.dot(ob_ref[...], w_ref[ka:, :], preferred_element_type=F32)
    y_ref[...] = x_ref[...] + acc


def _outproj_even(o1, o4, o16, l1, l4, l16, ob, w, x):
    T, D = x.shape
    tm = ROW_TILE
    half = o1.shape[1]
    row = lambda wd: pl.BlockSpec((tm, wd), lambda i: (i, 0))
    view = lambda d: pl.BlockSpec((tm // d, d * half), lambda i: (i, 0))
    d4, d16 = DILATION_GROUPS[1][1], DILATION_GROUPS[2][1]
    return pl.pallas_call(
        _outproj_even_kernel,
        grid=(T // tm,),
        in_specs=[row(half), view(d4), view(d16), row(half), view(d4), view(d16), row(half),
                  pl.BlockSpec(w.shape, lambda i: (0, 0)), row(D)],
        out_specs=row(D),
        out_shape=jax.ShapeDtypeStruct((T, D), F32),
        scratch_shapes=[pltpu.VMEM((half // LANES, tm, LANES), F32)] * 4,
        compiler_params=_cparams(("parallel",)),
        name="outproj_even",
    )(o1, o4, o16, l1, l4, l16, ob, w, x)


def _outproj_kernel(o_ref, w_ref, x_ref, y_ref):
    y_ref[...] = x_ref[...] + jnp.dot(o_ref[...], w_ref[...], preferred_element_type=F32)


def _outproj(o, w, x):
    T, D = x.shape
    tm = ROW_TILE
    row = lambda wd: pl.BlockSpec((tm, wd), lambda i: (i, 0))
    return pl.pallas_call(
        _outproj_kernel,
        grid=(T // tm,),
        in_specs=[row(o.shape[1]), pl.BlockSpec(w.shape, lambda i: (0, 0)), row(D)],
        out_specs=row(D),
        out_shape=jax.ShapeDtypeStruct((T, D), F32),
        compiler_params=_cparams(("parallel",)),
        name="outproj",
    )(o, w, x)


def _diff_kernel(lam_ref, qT_ref, k_ref, vT_ref, sub_ref, o_ref, s0_scr, s1_scr, acc_scr, *,
                 tq, tk, S, lam_init, unroll):
    row = lax.broadcasted_iota(jnp.int32, (LANES, 1), 0)
    comp0 = row < HEAD_DIM
    lamv = lam_ref[...]
    lam = (jnp.exp(jnp.sum(lamv[0:1] * lamv[1:2], axis=-1, keepdims=True))
           - jnp.exp(jnp.sum(lamv[2:3] * lamv[3:4], axis=-1, keepdims=True)) + lam_init)
    qT = qT_ref[0, 0]
    zero = jnp.zeros_like(qT)
    qqT = jnp.concatenate([jnp.where(comp0, qT, zero), jnp.where(comp0, zero, qT)], axis=1)
    ones = jnp.ones((DIFF_ONES_ROWS, tk), BF16)
    n_kv = S // tk

    def scores(t, dst):
        kb = k_ref[0, pl.ds(pl.multiple_of(t * tk, tk), tk), :]
        dst[...] = jnp.dot(kb, qqT, preferred_element_type=F32)

    def softmax_pv(src, t, m):
        s = src[...]
        lhs = jnp.concatenate([vT_ref[0, 0, t], ones], axis=0)
        m_new = jnp.maximum(m, jnp.max(s, axis=0, keepdims=True))
        alpha = jnp.exp2(m - m_new)
        p = jnp.exp2(s - m_new).astype(BF16)
        acc_scr[...] = alpha * acc_scr[...] + jnp.dot(lhs, p, preferred_element_type=F32)
        return m_new

    def step(i, m):
        bufs = (s0_scr, s1_scr)
        for u in range(unroll):
            t = unroll * i + u
            scores(jnp.minimum(t + 1, n_kv - 1), bufs[(u + 1) % 2])
            m = softmax_pv(bufs[u % 2], t, m)
        return m

    acc_scr[...] = jnp.zeros_like(acc_scr)
    scores(0, s0_scr)
    lax.fori_loop(0, n_kv // unroll, step, jnp.full((1, 2 * tq), NEG_INF, F32))
    acc = acc_scr[...]
    a = acc[:LANES] / acc[LANES:LANES + 1]
    o = a[:, :tq] - lam * a[:, tq:]
    ms = jnp.mean(o * o, axis=0, keepdims=True)
    o = o * lax.rsqrt(ms + EPS) * sub_ref[...] * (1.0 - lam_init)
    o_ref[0] = o.T.astype(o_ref.dtype)


def _diff_attention(lam_vecs, qT, k, vT, subln_col, lam_init):
    B, H, _, S = qT.shape
    tq, tk = DIFF_TQ, DIFF_TK
    unroll = min(DIFF_UNROLL, S // tk)
    assert S % tq == 0 and S % (tk * unroll) == 0 and unroll % 2 == 0
    acc_rows = LANES + DIFF_ONES_ROWS
    return pl.pallas_call(
        functools.partial(_diff_kernel, tq=tq, tk=tk, S=S, lam_init=lam_init, unroll=unroll),
        grid=(B, H, S // tq),
        in_specs=[
            pl.BlockSpec(lam_vecs.shape, lambda b, h, i: (0, 0)),
            pl.BlockSpec((1, 1, LANES, tq), lambda b, h, i: (b, h, 0, i)),
            pl.BlockSpec((1, S, LANES), lambda b, h, i: (b, 0, h)),
            pl.BlockSpec((1, 1, S // tk, LANES, tk), lambda b, h, i: (b, h, 0, 0, 0)),
            pl.BlockSpec((LANES, 1), lambda b, h, i: (0, 0)),
        ],
        out_specs=pl.BlockSpec((1, tq, LANES), lambda b, h, i: (b, i, h)),
        out_shape=jax.ShapeDtypeStruct((B, S, H * LANES), BF16),
        scratch_shapes=[pltpu.VMEM((tk, 2 * tq), F32), pltpu.VMEM((tk, 2 * tq), F32),
                        pltpu.VMEM((acc_rows, 2 * tq), F32)],
        compiler_params=_cparams(("parallel", "parallel", "arbitrary")),
        name="diff_attn",
    )(lam_vecs, qT, k, vT, subln_col)


def _router_kernel(x_ref, g_ref, whi_ref, wlo_ref, b_ref, xn_ref, gate_ref, eid_ref, cnt_ref):
    x = x_ref[...]
    ms = jnp.mean(x * x, axis=-1, keepdims=True)
    xn = x * lax.rsqrt(ms + EPS) * g_ref[...]
    xhi = xn.astype(BF16)
    xn_ref[...] = xhi
    xlo = (xn - xhi.astype(F32)).astype(BF16)
    whi = whi_ref[...]
    logits = (jnp.dot(xhi, whi, preferred_element_type=F32)
              + jnp.dot(xlo, whi, preferred_element_type=F32)
              + jnp.dot(xhi, wlo_ref[...], preferred_element_type=F32)) + b_ref[...]
    lane = _lane_iota(logits.shape).astype(F32)
    big = float(4 * LANES)
    lg = jnp.where(lane < N_GROUPS, logits, NEG_INF)
    mg = jnp.max(lg, axis=-1, keepdims=True)
    g_sel = jnp.min(jnp.where(lg == mg, lane, big), axis=-1, keepdims=True)
    p_sel = 1.0 / jnp.sum(jnp.exp(lg - mg), axis=-1, keepdims=True)
    e_lo = ROUTER_E0 + g_sel * EXPERTS_PER_GROUP
    le = jnp.where((lane >= e_lo) & (lane < e_lo + EXPERTS_PER_GROUP), logits, NEG_INF)
    v1 = jnp.max(le, axis=-1, keepdims=True)
    i1 = jnp.min(jnp.where(le == v1, lane, big), axis=-1, keepdims=True)
    le2 = jnp.where(lane == i1, NEG_INF, le)
    v2 = jnp.max(le2, axis=-1, keepdims=True)
    i2 = jnp.min(jnp.where(le2 == v2, lane, big), axis=-1, keepdims=True)
    e2 = jnp.exp(v2 - v1)
    den = 1.0 + e2
    g1 = (1.0 / den) * p_sel
    g2 = (e2 / den) * p_sel
    gate_ref[...] = jnp.where(lane == 0, g1, jnp.where(lane == 1, g2, 0.0))
    eid = jnp.where(lane == 0, i1 - ROUTER_E0, jnp.where(lane == 1, i2 - ROUTER_E0, 0.0))
    eid_ref[...] = eid.astype(jnp.int32)
    chosen = ((lane == i1) | (lane == i2)).astype(F32)

    @pl.when(pl.program_id(0) == 0)
    def _():
        cnt_ref[...] = jnp.zeros_like(cnt_ref)

    cnt_ref[...] += jnp.sum(chosen, axis=0, keepdims=True)


def _router(x, g, whi, wlo, bias):
    T, D = x.shape
    tm = ROW_TILE
    const = lambda i: (0, 0)
    row = lambda wd: pl.BlockSpec((tm, wd), lambda i: (i, 0))
    return pl.pallas_call(
        _router_kernel,
        grid=(T // tm,),
        in_specs=[row(D), pl.BlockSpec((1, D), const), pl.BlockSpec((D, ROUTER_COLS), const),
                  pl.BlockSpec((D, ROUTER_COLS), const), pl.BlockSpec((1, ROUTER_COLS), const)],
        out_specs=[row(D), row(ROUTER_COLS), row(ROUTER_COLS), pl.BlockSpec((1, ROUTER_COLS), const)],
        out_shape=[jax.ShapeDtypeStruct((T, D), BF16), jax.ShapeDtypeStruct((T, ROUTER_COLS), F32),
                   jax.ShapeDtypeStruct((T, ROUTER_COLS), jnp.int32),
                   jax.ShapeDtypeStruct((1, ROUTER_COLS), F32)],
        compiler_params=_cparams(("arbitrary",)),
        name="router",
    )(x, g, whi, wlo, bias)


def _slot_kernel(eid_ref, pstart_ref, slot_ref, run_ref):
    tm = eid_ref.shape[0]

    @pl.when(pl.program_id(0) == 0)
    def _():
        run_ref[...] = jnp.zeros_like(run_ref)

    eid = eid_ref[...]
    lane = _lane_iota(eid.shape)
    oh0 = lane == (eid[:, 0:1] + ROUTER_E0)
    oh1 = lane == (eid[:, 1:2] + ROUTER_E0)
    both = (oh0 | oh1).astype(BF16)
    r = lax.broadcasted_iota(jnp.int32, (tm, tm), 0)
    c = lax.broadcasted_iota(jnp.int32, (tm, tm), 1)
    earlier = (c < r).astype(BF16)
    before = jnp.dot(earlier, both, preferred_element_type=F32)
    base = pstart_ref[...] + run_ref[...] + before
    s0 = jnp.sum(jnp.where(oh0, base, 0.0), axis=-1, keepdims=True)
    s1 = jnp.sum(jnp.where(oh1, base, 0.0), axis=-1, keepdims=True)
    slot_ref[...] = jnp.where(lane == 0, s0, jnp.where(lane == 1, s1, 0.0)).astype(jnp.int32)
    run_ref[...] += jnp.sum(both.astype(F32), axis=0, keepdims=True)


def _slots(eid, pstart_row):
    T = eid.shape[0]
    tm = ROW_TILE
    return pl.pallas_call(
        _slot_kernel,
        grid=(T // tm,),
        in_specs=[pl.BlockSpec((tm, ROUTER_COLS), lambda i: (i, 0)),
                  pl.BlockSpec((1, ROUTER_COLS), lambda i: (0, 0))],
        out_specs=pl.BlockSpec((tm, ROUTER_COLS), lambda i: (i, 0)),
        out_shape=jax.ShapeDtypeStruct((T, ROUTER_COLS), jnp.int32),
        scratch_shapes=[pltpu.VMEM((1, ROUTER_COLS), F32)],
        compiler_params=_cparams(("arbitrary",)),
        name="slots",
    )(eid, pstart_row)


def _expert_kernel(te_ref, nt_ref, xs_ref, wg_ref, wu_ref, wd_ref, y_ref):
    t = pl.program_id(0)

    @pl.when(t < nt_ref[0])
    def _():
        xs = xs_ref[...]
        a = jnp.dot(xs, wg_ref[0], preferred_element_type=F32)
        u = jnp.dot(xs, wu_ref[0], preferred_element_type=F32)
        h = (a * jax.nn.sigmoid(a) * u).astype(BF16)
        y_ref[...] = jnp.dot(h, wd_ref[0], preferred_element_type=F32)

    @pl.when(t >= nt_ref[0])
    def _():
        y_ref[...] = jnp.zeros_like(y_ref)


def _experts(tile_expert, n_tiles, xs, wg, wu, wd):
    Mp, D = xs.shape
    te = EXPERT_TILE
    grid_spec = pltpu.PrefetchScalarGridSpec(
        num_scalar_prefetch=2,
        grid=(Mp // te,),
        in_specs=[
            pl.BlockSpec((te, D), lambda t, te_r, nt_r: (t, 0)),
            pl.BlockSpec((1, D, D_EXPERT), lambda t, te_r, nt_r: (te_r[t], 0, 0)),
            pl.BlockSpec((1, D, D_EXPERT), lambda t, te_r, nt_r: (te_r[t], 0, 0)),
            pl.BlockSpec((1, D_EXPERT, D), lambda t, te_r, nt_r: (te_r[t], 0, 0)),
        ],
        out_specs=pl.BlockSpec((te, D), lambda t, te_r, nt_r: (t, 0)),
    )
    return pl.pallas_call(
        _expert_kernel,
        grid_spec=grid_spec,
        out_shape=jax.ShapeDtypeStruct((Mp, D), F32),
        compiler_params=_cparams(("arbitrary",)),
        name="experts",
    )(tile_expert, n_tiles, xs, wg, wu, wd)


def _combine_kernel(x_ref, y0_ref, y1_ref, gate_ref, o_ref):
    gate = gate_ref[...]
    o_ref[...] = x_ref[...] + gate[:, 0:1] * y0_ref[...] + gate[:, 1:2] * y1_ref[...]


def _combine(x, y0, y1, gate):
    T, D = x.shape
    tm = ROW_TILE
    row = lambda wd: pl.BlockSpec((tm, wd), lambda i: (i, 0))
    return pl.pallas_call(
        _combine_kernel,
        grid=(T // tm,),
        in_specs=[row(D), row(D), row(D), row(ROUTER_COLS)],
        out_specs=row(D),
        out_shape=jax.ShapeDtypeStruct((T, D), F32),
        compiler_params=_cparams(("parallel",)),
        name="moe_combine",
    )(x, y0, y1, gate)


def _moe(x, p):
    T, D = x.shape
    xn, gate, eid, counts_row = _router(x, p["ffn_norm"], p["r_whi"], p["r_wlo"], p["r_bias"])
    M = T * TOP_K
    te = EXPERT_TILE
    Mp = M + N_EXPERTS * te
    counts = counts_row[0, ROUTER_E0:ROUTER_E0 + N_EXPERTS].astype(jnp.int32)
    padded = ((counts + te - 1) // te) * te
    pend = jnp.cumsum(padded)
    pstart = pend - padded
    pstart_row = jnp.zeros((1, ROUTER_COLS), F32).at[0, ROUTER_E0:ROUTER_E0 + N_EXPERTS].set(pstart.astype(F32))
    slot = _slots(eid, pstart_row)[:, :TOP_K]
    tok_of_slot = jnp.zeros((Mp,), jnp.int32).at[slot.reshape(M)].set(
        jnp.arange(M, dtype=jnp.int32) // TOP_K)
    tile_ids = jnp.arange(Mp // te, dtype=jnp.int32)
    tile_expert = jnp.minimum(jnp.searchsorted(pend // te, tile_ids, side="right"),
                              N_EXPERTS - 1).astype(jnp.int32)
    n_tiles = (pend[-1] // te).astype(jnp.int32).reshape(1)
    xs = jnp.take(xn, tok_of_slot, axis=0)
    ys = _experts(tile_expert, n_tiles, xs, p["w_gate"], p["w_up"], p["w_down"])
    y0 = jnp.take(ys, slot[:, 0], axis=0)
    y1 = jnp.take(ys, slot[:, 1], axis=0)
    return _combine(x, y0, y1, gate)


def _rope_tables(seq):
    half = HEAD_DIM // 2
    inv = ROPE_THETA ** (-jnp.arange(half, dtype=F32) / half)
    ang = jnp.arange(seq, dtype=F32)[:, None] * inv[None, :]
    cos = jnp.tile(jnp.cos(ang), (1, LANES // half))
    sin = jnp.sin(ang)
    sin = jnp.tile(jnp.concatenate([-sin, sin], axis=1), (1, LANES // HEAD_DIM))
    return cos, sin


def _head_gain(g, heads):
    return jnp.tile(g.astype(F32), (heads,))


def _prepare(ev_norm, ev_w_in, ev_qn_a, ev_kn_a, ev_qn_b, ev_kn_b, ev_sink, ev_w_out,
             od_norm, od_w_in, od_qn, od_kn, od_lq1, od_lk1, od_lq2, od_lk2, od_subln, od_w_out,
             ffn_norm, rg_w, rg_b, re_w, re_b, w_gate, w_up, w_down):
    hd = HEAD_DIM
    P = {}
    r = jnp.arange(MXU_DIM) // hd
    P["bd"] = (r[:, None] == r[None, :]).astype(BF16)
    w = ev_w_in[0]
    a3 = 3 * A_HEADS * hd
    qb_end = a3 + B_Q_HEADS * hd
    kb = [w[:, qb_end + h * hd:qb_end + (h + 1) * hd] for h in range(B_KV_HEADS)]
    vb0 = qb_end + B_KV_HEADS * hd
    vb = [w[:, vb0 + h * hd:vb0 + (h + 1) * hd] for h in range(B_KV_HEADS)]
    P["ev_w_in"] = jnp.concatenate([w[:, :qb_end]] + [c for h in kb for c in (h, h)]
                                   + [c for h in vb for c in (h, h)], axis=1).astype(BF16)
    ones = jnp.ones((A_HEADS * hd,), F32)
    P["ev_hg"] = jnp.concatenate([
        _head_gain(ev_qn_a[0], A_HEADS), _head_gain(ev_kn_a[0], A_HEADS), ones,
        _head_gain(ev_qn_b[0], B_Q_HEADS), _head_gain(ev_kn_b[0], 2 * B_KV_HEADS),
        jnp.ones((2 * B_KV_HEADS * hd,), F32)])[None, :]
    P["ev_norm"] = ev_norm[0][None, :]
    P["ev_sink"] = ev_sink[0].astype(F32)
    P["ev_w_out"] = ev_w_out[0].astype(BF16)
    P["od_w_in"] = od_w_in[0].astype(BF16)
    P["od_hg"] = jnp.concatenate([_head_gain(od_qn[0], 2 * C_HEADS), _head_gain(od_kn[0], 2 * C_HEADS),
                                  jnp.ones((2 * C_HEADS * hd,), F32)])[None, :]
    P["od_norm"] = od_norm[0][None, :]
    P["od_lam"] = jnp.stack([od_lq1[0], od_lk1[0], od_lq2[0], od_lk2[0]]).astype(F32)
    P["od_subln"] = od_subln[0][:, None].astype(F32)
    P["od_w_out"] = od_w_out[0].astype(BF16)
    P["moe"] = []
    for layer in range(ffn_norm.shape[0]):
        wr = jnp.zeros((D_MODEL, ROUTER_COLS), F32)
        wr = wr.at[:, :N_GROUPS].set(rg_w[layer])
        we = jnp.transpose(re_w[layer], (1, 0, 2)).reshape(D_MODEL, N_EXPERTS)
        wr = wr.at[:, ROUTER_E0:ROUTER_E0 + N_EXPERTS].set(we)
        bias = jnp.zeros((ROUTER_COLS,), F32).at[:N_GROUPS].set(rg_b[layer])
        bias = bias.at[ROUTER_E0:ROUTER_E0 + N_EXPERTS].set(re_b[layer].reshape(N_EXPERTS))
        whi = wr.astype(BF16)
        P["moe"].append({
            "ffn_norm": ffn_norm[layer][None, :],
            "r_whi": whi, "r_wlo": (wr - whi.astype(F32)).astype(BF16), "r_bias": bias[None, :],
            "w_gate": w_gate[layer].astype(BF16), "w_up": w_up[layer].astype(BF16),
            "w_down": w_down[layer].astype(BF16),
        })
    return P


def _even_mixer(x, P, B, S):
    T = B * S
    hd = HEAD_DIM
    aw = A_HEADS * hd
    cos_t, sin_t = _rope_tables(S)
    scale = hd ** -0.5
    tm = ROW_TILE
    dils = [d for _, d in DILATION_GROUPS]
    assert dils[0] == 1
    out_blocks = [(tm, aw)] * 4 + [(tm, 2 * LANES)] * 2
    views = []
    for _ in range(3):
        tv = []
        for d in dils[1:]:
            tv.append((d, len(out_blocks)))
            out_blocks.append((tm // d, d * aw))
        views.append(tuple(tv))
    plan = ((0, 0, aw, True, scale, views[0]), (1, aw, aw, True, 1.0, views[1]),
            (2, 2 * aw, aw, False, 1.0, views[2]), (3, 3 * aw, aw, True, scale, ()),
            (4, 4 * aw, 2 * LANES, True, 1.0, ()), (5, 4 * aw + 2 * LANES, 2 * LANES, False, 1.0, ()))
    res = _inproj(x, P["ev_norm"], P["ev_w_in"], P["ev_hg"], cos_t, sin_t, P["bd"], plan, out_blocks, S,
                  stage_width=aw)
    qb, kb, vb = res[3:6]
    outs, lses = [], []
    for gi, (window, dil) in enumerate(DILATION_GROUPS):
        L = S // dil
        qkv = [res[t] if dil == 1 else res[views[t][gi - 1][1]] for t in range(3)]
        o, lse = _band_a(*[t.reshape(B, L, dil * aw) for t in qkv], window // (2 * dil))
        outs.append(o.reshape(T // dil, dil * aw))
        lses.append(lse.reshape(T // dil, dil * aw))
    ob = _band_b(P["ev_sink"], qb.reshape(B, S, aw), kb.reshape(B, S, 2 * LANES), vb.reshape(B, S, 2 * LANES))
    return _outproj_even(outs[0], outs[1], outs[2], lses[0], lses[1], lses[2], ob.reshape(T, aw),
                         P["ev_w_out"], x)


def _diff_mixer(x, P, B, S, lam_init):
    T = B * S
    hd = HEAD_DIM
    cw = 2 * C_HEADS * hd
    cos_t, sin_t = _rope_tables(S)
    plan = ((0, 0, cw, True, LOG2_E * hd ** -0.5, ()), (1, cw, cw, True, 1.0, ()), (2, 2 * cw, cw, False, 1.0, ()))
    q, k, v = _inproj(x, P["od_norm"], P["od_w_in"], P["od_hg"], cos_t, sin_t, P["bd"], plan,
                      [(ROW_TILE, cw)] * 3, S)
    tk = DIFF_TK
    qT = q.reshape(B, S, C_HEADS, LANES).transpose(0, 2, 3, 1)
    vT = v.reshape(B, S // tk, tk, C_HEADS, LANES).transpose(0, 3, 1, 4, 2)
    o = _diff_attention(P["od_lam"], qT, k.reshape(B, S, cw), vT, P["od_subln"], lam_init)
    return _outproj(o.reshape(T, cw), P["od_w_out"], x)


def _trunk(x3, P):
    B, S, D = x3.shape
    x = x3.reshape(B * S, D)
    x = _even_mixer(x, P, B, S)
    x = _moe(x, P["moe"][0])
    lam_init = 0.8 - 0.6 * math.exp(-0.3 * 1)
    x = _diff_mixer(x, P, B, S, lam_init)
    x = _moe(x, P["moe"][1])
    return x.reshape(B, S, D)


def kernel(x_prompt, x_sample, ev_norm, ev_w_in, ev_qn_a, ev_kn_a, ev_qn_b, ev_kn_b, ev_sink, ev_w_out,
           od_norm, od_w_in, od_qn, od_kn, od_lq1, od_lk1, od_lq2, od_lk2, od_subln, od_w_out,
           ffn_norm, rg_w, rg_b, re_w, re_b, w_gate, w_up, w_down):
    P = _prepare(ev_norm, ev_w_in, ev_qn_a, ev_kn_a, ev_qn_b, ev_kn_b, ev_sink, ev_w_out,
                 od_norm, od_w_in, od_qn, od_kn, od_lq1, od_lk1, od_lq2, od_lk2, od_subln, od_w_out,
                 ffn_norm, rg_w, rg_b, re_w, re_b, w_gate, w_up, w_down)
    return (_trunk(x_prompt, P), _trunk(x_sample, P))
```

```python
import functools
import math

import jax
import jax.numpy as jnp
from jax import lax
from jax.experimental import pallas as pl
from jax.experimental.pallas import tpu as pltpu

F32 = jnp.float32
BF16 = jnp.bfloat16

D_MODEL = 1024
HEAD_DIM = 64
A_HEADS = 8
DILATION_GROUPS = ((128, 1), (512, 4), (2048, 16))
B_Q_HEADS = 8
B_KV_HEADS = 2
B_HALF_WINDOW = 128
C_HEADS = 8
N_GROUPS = 4
EXPERTS_PER_GROUP = 8
N_EXPERTS = N_GROUPS * EXPERTS_PER_GROUP
TOP_K = 2
D_EXPERT = 512
ROPE_THETA = 10000.0
EPS = 1e-6
NEG_INF = -1e30

LANES = 128
MXU_DIM = 256
VMEM_LIMIT_BYTES = 52 * 1024 * 1024

ROW_TILE = 512
COL_CHUNK = 256
BAND_TQ = 128
BAND_UNROLL = 2
BAND_BLOCK_BYTES =2 * 1024 * 1024
DIFF_TQ = 512
DIFF_TK = 256
DIFF_UNROLL = 8
DIFF_ONES_ROWS = 16
LOG2_E = 1.4426950408889634
EXPERT_TILE = 256
ROUTER_COLS = 128
ROUTER_E0 = 8


def _cparams(sem):
    return pltpu.CompilerParams(dimension_semantics=sem, vmem_limit_bytes=VMEM_LIMIT_BYTES)


def _lane_iota(shape):
    return lax.broadcasted_iota(jnp.int32, shape, len(shape) - 1)


def _inproj_kernel(x_ref, g_ref, w_ref, hg_ref, cos_ref, sin_ref, bd_ref, *refs, plan, n_out):
    out_refs = refs[:n_out]
    stage_ref = refs[n_out] if len(refs) > n_out else None
    x = x_ref[...]
    tm = x.shape[0]
    ms = jnp.mean(x * x, axis=-1, keepdims=True)
    xn = (x * lax.rsqrt(ms + EPS) * g_ref[...]).astype(BF16)
    lane = _lane_iota((1, LANES))
    first_half = (lane % HEAD_DIM) < (HEAD_DIM // 2)
    cos = cos_ref[...]
    sin = sin_ref[...]
    for oi, c0, width, normed, scale, views in plan:
        for cc in range(0, width, COL_CHUNK):
            cw = min(COL_CHUNK, width - cc)
            y = jnp.dot(xn, w_ref[:, c0 + cc:c0 + cc + cw], preferred_element_type=F32)
            if normed:
                hs = jnp.dot((y * y).astype(BF16), bd_ref[:cw, :cw], preferred_element_type=F32)
                y = y * lax.rsqrt(hs * (1.0 / HEAD_DIM) + EPS) * hg_ref[:, c0 + cc:c0 + cc + cw]
            for j in range(cw // LANES):
                r = y[:, j * LANES:(j + 1) * LANES]
                if normed:
                    rot = jnp.where(first_half, pltpu.roll(r, LANES - HEAD_DIM // 2, 1),
                                    pltpu.roll(r, HEAD_DIM // 2, 1))
                    r = r * cos + rot * sin
                    if scale != 1.0:
                        r = r * scale
                cols = slice(cc + j * LANES, cc + (j + 1) * LANES)
                out_refs[oi][:, cols] = r.astype(BF16)
                if views:
                    stage_ref[cc // LANES + j] = r
        for d, vi in views:
            for rr in range(d):
                for j in range(width // LANES):
                    out_refs[vi][:, rr * width + j * LANES:rr * width + (j + 1) * LANES] = (
                        stage_ref[j, pl.ds(rr, tm // d, stride=d), :].astype(BF16))


def _inproj(x, g, w, hg, cos_t, sin_t, bd, plan, out_blocks, seq, stage_width=0):
    T, D = x.shape
    N = w.shape[1]
    tm = ROW_TILE
    n_pos_blocks = seq // tm
    const = lambda i: (0, 0)
    scratch = [pltpu.VMEM((stage_width // LANES, tm, LANES), F32)] if stage_width else []
    return pl.pallas_call(
        functools.partial(_inproj_kernel, plan=plan, n_out=len(out_blocks)),
        grid=(T // tm,),
        in_specs=[
            pl.BlockSpec((tm, D), lambda i: (i, 0)),
            pl.BlockSpec((1, D), const),
            pl.BlockSpec((D, N), const),
            pl.BlockSpec((1, N), const),
            pl.BlockSpec((tm, LANES), lambda i: (i % n_pos_blocks, 0)),
            pl.BlockSpec((tm, LANES), lambda i: (i % n_pos_blocks, 0)),
            pl.BlockSpec((MXU_DIM, MXU_DIM), const),
        ],
        out_specs=[pl.BlockSpec((br, bc), lambda i: (i, 0)) for br, bc in out_blocks],
        out_shape=[jax.ShapeDtypeStruct((T // tm * br, bc), BF16) for br, bc in out_blocks],
        scratch_shapes=scratch,
        compiler_params=_cparams(("parallel",)),
        name="inproj",
    )(x, g, w, hg, cos_t, sin_t, bd)


def _band_window(i, tq, hw, L):
    win = tq + 2 * hw
    start = jnp.clip(i * tq - hw, 0, L - win)
    start = pl.multiple_of(start, 64)
    qpos = i * tq + lax.broadcasted_iota(jnp.int32, (tq, win), 0)
    kpos = start + lax.broadcasted_iota(jnp.int32, (tq, win), 1)
    valid = jnp.abs(kpos - qpos) <= hw
    return start, win, valid


def _band_a_kernel(q_ref, k_ref, v_ref, o_ref, lse_ref, *, hw, tq, L, wb):
    lane = _lane_iota((1, LANES))
    head0 = lane < HEAD_DIM

    def qblock(i, carry):
        start, win, valid = _band_window(i, tq, hw, L)
        rows = pl.ds(pl.multiple_of(i * tq, tq), tq)
        for sl in range(wb // LANES):
            cols = slice(sl * LANES, (sl + 1) * LANES)
            q = q_ref[0, rows, cols]
            kw = k_ref[0, pl.ds(start, win), cols]
            vw = v_ref[0, pl.ds(start, win), cols]
            outs = []
            lses = []
            for hmask in (head0, jnp.logical_not(head0)):
                qh = jnp.where(hmask, q, jnp.zeros_like(q))
                s = lax.dot_general(qh, kw, (((1,), (1,)), ((), ())), preferred_element_type=F32)
                s = jnp.where(valid, s, NEG_INF)
                m = jnp.max(s, axis=-1, keepdims=True)
                p = jnp.exp(s - m)
                den = jnp.sum(p, axis=-1, keepdims=True)
                pv = jnp.dot(p.astype(BF16), vw, preferred_element_type=F32)
                outs.append(pv / den)
                lses.append(m + jnp.log(den))
            o_ref[0, rows, cols] = jnp.where(head0, outs[0], outs[1]).astype(o_ref.dtype)
            lse_ref[0, rows, cols] = jnp.where(head0, lses[0], lses[1])
        return carry

    lax.fori_loop(0, L // tq, qblock, 0, unroll=BAND_UNROLL)


def _band_a(q, k, v, hw):
    NB, L, C = q.shape
    tq = BAND_TQ
    assert L >= tq + 2 * hw and L % tq == 0
    wb = min(C, A_HEADS * HEAD_DIM)
    while L * wb * 2 > BAND_BLOCK_BYTES and wb > LANES:
        wb //= 2
    spec = pl.BlockSpec((1, L, wb), lambda n, c: (n, 0, c))
    return pl.pallas_call(
        functools.partial(_band_a_kernel, hw=hw, tq=tq, L=L, wb=wb),
        grid=(NB, C // wb),
        in_specs=[spec, spec, spec],
        out_specs=[spec, spec],
        out_shape=[jax.ShapeDtypeStruct((NB, L, C), BF16), jax.ShapeDtypeStruct((NB, L, C), F32)],
        compiler_params=_cparams(("parallel", "parallel")),
        name="band_a",
    )(q, k, v)


def _band_b_kernel(sink_ref, q_ref, k_ref, v_ref, o_ref, *, hw, tq, L):
    lane = _lane_iota((1, LANES))
    head0 = lane < HEAD_DIM
    hk = pl.program_id(1)
    group = B_Q_HEADS // B_KV_HEADS
    sink = jnp.concatenate(
        [jnp.full((tq, 1), sink_ref[hk * group + g], F32) for g in range(group)], axis=0)

    def qblock(i, carry):
        start, win, valid = _band_window(i, tq, hw, L)
        rows = pl.ds(pl.multiple_of(i * tq, tq), tq)
        kw = k_ref[0, pl.ds(start, win), :]
        vw = v_ref[0, pl.ds(start, win), :]
        qs = []
        for g in range(group):
            q = q_ref[0, rows, (g // 2) * LANES:(g // 2 + 1) * LANES]
            hmask = head0 if g % 2 == 0 else jnp.logical_not(head0)
            qs.append(jnp.where(hmask, q, jnp.zeros_like(q)))
        qq = jnp.concatenate(qs, axis=0)
        s = lax.dot_general(qq, kw, (((1,), (1,)), ((), ())), preferred_element_type=F32)
        valid4 = jnp.concatenate([valid] * group, axis=0)
        s = jnp.where(valid4, s, NEG_INF)
        m = jnp.maximum(jnp.max(s, axis=-1, keepdims=True), sink)
        p = jnp.exp(s - m)
        den = jnp.sum(p, axis=-1, keepdims=True) + jnp.exp(sink - m)
        pv = jnp.dot(p.astype(BF16), vw, preferred_element_type=F32) / den
        for pair in range(group // 2):
            a = pv[(2 * pair) * tq:(2 * pair + 1) * tq]
            b = pv[(2 * pair + 1) * tq:(2 * pair + 2) * tq]
            o_ref[0, rows, pair * LANES:(pair + 1) * LANES] = jnp.where(head0, a, b).astype(o_ref.dtype)
        return carry

    lax.fori_loop(0, L // tq, qblock, 0, unroll=BAND_UNROLL)


def _band_b(sink, q, kdup, vdup):
    B, S, C = q.shape
    tq = BAND_TQ
    hw = B_HALF_WINDOW
    qw = C // B_KV_HEADS
    assert S >= tq + 2 * hw
    return pl.pallas_call(
        functools.partial(_band_b_kernel, hw=hw, tq=tq, L=S),
        grid=(B, B_KV_HEADS),
        in_specs=[
            pl.BlockSpec(memory_space=pltpu.SMEM),
            pl.BlockSpec((1, S, qw), lambda b, h: (b, 0, h)),
            pl.BlockSpec((1, S, LANES), lambda b, h: (b, 0, h)),
            pl.BlockSpec((1, S, LANES), lambda b, h: (b, 0, h)),
        ],
        out_specs=pl.BlockSpec((1, S, qw), lambda b, h: (b, 0, h)),
        out_shape=jax.ShapeDtypeStruct((B, S, C), BF16),
        compiler_params=_cparams(("parallel", "parallel")),
        name="band_b",
    )(sink, q, kdup, vdup)


def _outproj_even_kernel(o1_ref, o4_ref, o16_ref, l1_ref, l4_ref, l16_ref, ob_ref, w_ref, x_ref, y_ref,
                         so4, sl4, so16, sl16):
    tm, half = o1_ref.shape

    def token_major(src_ref, stage_ref, d):
        for rr in range(d):
            for j in range(half // LANES):
                stage_ref[j, pl.ds(rr, tm // d, stride=d), :] = (
                    src_ref[:, rr * half + j * LANES:rr * half + (j + 1) * LANES].astype(F32))
        return jnp.concatenate([stage_ref[j] for j in range(half // LANES)], axis=1)

    dils = [d for _, d in DILATION_GROUPS]
    l1 = l1_ref[...]
    o4, l4 = token_major(o4_ref, so4, dils[1]), token_major(l4_ref, sl4, dils[1])
    o16, l16 = token_major(o16_ref, so16, dils[2]), token_major(l16_ref, sl16, dils[2])
    m = jnp.maximum(jnp.maximum(l1, l4), l16)
    e1, e4, e16 = jnp.exp(l1 - m), jnp.exp(l4 - m), jnp.exp(l16 - m)
    num = e1 * o1_ref[...].astype(F32) + e4 * o4 + e16 * o16
    oa = (num / (e1 + e4 + e16)).astype(BF16)
    ka = oa.shape[1]
    acc = jnp.dot(oa, w_ref[:ka, :], preferred_element_type=F32)
    acc = acc + jnp.dot(ob_ref[...], w_ref[ka:, :], preferred_element_type=F32)
    y_ref[...] = x_ref[...] + acc


def _outproj_even(o1, o4, o16, l1, l4, l16, ob, w, x):
    T, D = x.shape
    tm = ROW_TILE
    half = o1.shape[1]
    row = lambda wd: pl.BlockSpec((tm, wd), lambda i: (i, 0))
    view = lambda d: pl.BlockSpec((tm // d, d * half), lambda i: (i, 0))
    d4, d16 = DILATION_GROUPS[1][1], DILATION_GROUPS[2][1]
    return pl.pallas_call(
        _outproj_even_kernel,
        grid=(T // tm,),
        in_specs=[row(half), view(d4), view(d16), row(half), view(d4), view(d16), row(half),
                  pl.BlockSpec(w.shape, lambda i: (0, 0)), row(D)],
        out_specs=row(D),
        out_shape=jax.ShapeDtypeStruct((T, D), F32),
        scratch_shapes=[pltpu.VMEM((half // LANES, tm, LANES), F32)] * 4,
        compiler_params=_cparams(("parallel",)),
        name="outproj_even",
    )(o1, o4, o16, l1, l4, l16, ob, w, x)


def _outproj_kernel(o_ref, w_ref, x_ref, y_ref):
    y_ref[...] = x_ref[...] + jnp.dot(o_ref[...], w_ref[...], preferred_element_type=F32)


def _outproj(o, w, x):
    T, D = x.shape
    tm = ROW_TILE
    row = lambda wd: pl.BlockSpec((tm, wd), lambda i: (i, 0))
    return pl.pallas_call(
        _outproj_kernel,
        grid=(T // tm,),
        in_specs=[row(o.shape[1]), pl.BlockSpec(w.shape, lambda i: (0, 0)), row(D)],
        out_specs=row(D),
        out_shape=jax.ShapeDtypeStruct((T, D), F32),
        compiler_params=_cparams(("parallel",)),
        name="outproj",
    )(o, w, x)


def _diff_kernel(lam_ref, qT_ref, k_ref, vT_ref, sub_ref, o_ref, s0_scr, s1_scr, acc_scr, *,
                 tq, tk, S, lam_init, unroll):
    row = lax.broadcasted_iota(jnp.int32, (LANES, 1), 0)
    comp0 = row < HEAD_DIM
    lamv = lam_ref[...]
    lam = (jnp.exp(jnp.sum(lamv[0:1] * lamv[1:2], axis=-1, keepdims=True))
           - jnp.exp(jnp.sum(lamv[2:3] * lamv[3:4], axis=-1, keepdims=True)) + lam_init)
    qT = qT_ref[0, 0]
    zero = jnp.zeros_like(qT)
    qqT = jnp.concatenate([jnp.where(comp0, qT, zero), jnp.where(comp0, zero, qT)], axis=1)
    ones = jnp.ones((DIFF_ONES_ROWS, tk), BF16)
    n_kv = S // tk

    def scores(t, dst):
        kb = k_ref[0, pl.ds(pl.multiple_of(t * tk, tk), tk), :]
        dst[...] = jnp.dot(kb, qqT, preferred_element_type=F32)

    def softmax_pv(src, t, m):
        s = src[...]
        lhs = jnp.concatenate([vT_ref[0, 0, t], ones], axis=0)
        m_new = jnp.maximum(m, jnp.max(s, axis=0, keepdims=True))
        alpha = jnp.exp2(m - m_new)
        p = jnp.exp2(s - m_new).astype(BF16)
        acc_scr[...] = alpha * acc_scr[...] + jnp.dot(lhs, p, preferred_element_type=F32)
        return m_new

    def step(i, m):
        bufs = (s0_scr, s1_scr)
        for u in range(unroll):
            t = unroll * i + u
            scores(jnp.minimum(t + 1, n_kv - 1), bufs[(u + 1) % 2])
            m = softmax_pv(bufs[u % 2], t, m)
        return m

    acc_scr[...] = jnp.zeros_like(acc_scr)
    scores(0, s0_scr)
    lax.fori_loop(0, n_kv // unroll, step, jnp.full((1, 2 * tq), NEG_INF, F32))
    acc = acc_scr[...]
    a = acc[:LANES] / acc[LANES:LANES + 1]
    o = a[:, :tq] - lam * a[:, tq:]
    ms = jnp.mean(o * o, axis=0, keepdims=True)
    o = o * lax.rsqrt(ms + EPS) * sub_ref[...] * (1.0 - lam_init)
    o_ref[0] = o.T.astype(o_ref.dtype)


def _diff_attention(lam_vecs, qT, k, vT, subln_col, lam_init):
    B, H, _, S = qT.shape
    tq, tk = DIFF_TQ, DIFF_TK
    unroll = min(DIFF_UNROLL, S // tk)
    assert S % tq == 0 and S % (tk * unroll) == 0 and unroll % 2 == 0
    acc_rows = LANES + DIFF_ONES_ROWS
    return pl.pallas_call(
        functools.partial(_diff_kernel, tq=tq, tk=tk, S=S, lam_init=lam_init, unroll=unroll),
        grid=(B, H, S // tq),
        in_specs=[
            pl.BlockSpec(lam_vecs.shape, lambda b, h, i: (0, 0)),
            pl.BlockSpec((1, 1, LANES, tq), lambda b, h, i: (b, h, 0, i)),
            pl.BlockSpec((1, S, LANES), lambda b, h, i: (b, 0, h)),
            pl.BlockSpec((1, 1, S // tk, LANES, tk), lambda b, h, i: (b, h, 0, 0, 0)),
            pl.BlockSpec((LANES, 1), lambda b, h, i: (0, 0)),
        ],
        out_specs=pl.BlockSpec((1, tq, LANES), lambda b, h, i: (b, i, h)),
        out_shape=jax.ShapeDtypeStruct((B, S, H * LANES), BF16),
        scratch_shapes=[pltpu.VMEM((tk, 2 * tq), F32), pltpu.VMEM((tk, 2 * tq), F32),
                        pltpu.VMEM((acc_rows, 2 * tq), F32)],
        compiler_params=_cparams(("parallel", "parallel", "arbitrary")),
        name="diff_attn",
    )(lam_vecs, qT, k, vT, subln_col)


def _router_kernel(x_ref, g_ref, whi_ref, wlo_ref, b_ref, xn_ref, gate_ref, eid_ref, cnt_ref):
    x = x_ref[...]
    ms = jnp.mean(x * x, axis=-1, keepdims=True)
    xn = x * lax.rsqrt(ms + EPS) * g_ref[...]
    xhi = xn.astype(BF16)
    xn_ref[...] = xn
    xlo = (xn - xhi.astype(F32)).astype(BF16)
    whi = whi_ref[...]
    logits = (jnp.dot(xhi, whi, preferred_element_type=F32)
              + jnp.dot(xlo, whi, preferred_element_type=F32)
              + jnp.dot(xhi, wlo_ref[...], preferred_element_type=F32)) + b_ref[...]
    lane = _lane_iota(logits.shape).astype(F32)
    big = float(4 * LANES)
    lg = jnp.where(lane < N_GROUPS, logits, NEG_INF)
    mg = jnp.max(lg, axis=-1, keepdims=True)
    g_sel = jnp.min(jnp.where(lg == mg, lane, big), axis=-1, keepdims=True)
    p_sel = 1.0 / jnp.sum(jnp.exp(lg - mg), axis=-1, keepdims=True)
    e_lo = ROUTER_E0 + g_sel * EXPERTS_PER_GROUP
    le = jnp.where((lane >= e_lo) & (lane < e_lo + EXPERTS_PER_GROUP), logits, NEG_INF)
    v1 = jnp.max(le, axis=-1, keepdims=True)
    i1 = jnp.min(jnp.where(le == v1, lane, big), axis=-1, keepdims=True)
    le2 = jnp.where(lane == i1, NEG_INF, le)
    v2 = jnp.max(le2, axis=-1, keepdims=True)
    i2 = jnp.min(jnp.where(le2 == v2, lane, big), axis=-1, keepdims=True)
    e2 = jnp.exp(v2 - v1)
    den = 1.0 + e2
    g1 = (1.0 / den) * p_sel
    g2 = (e2 / den) * p_sel
    gate_ref[...] = jnp.where(lane == 0, g1, jnp.where(lane == 1, g2, 0.0))
    eid = jnp.where(lane == 0, i1 - ROUTER_E0, jnp.where(lane == 1, i2 - ROUTER_E0, 0.0))
    eid_ref[...] = eid.astype(jnp.int32)
    chosen = ((lane == i1) | (lane == i2)).astype(F32)

    @pl.when(pl.program_id(0) == 0)
    def _():
        cnt_ref[...] = jnp.zeros_like(cnt_ref)

    cnt_ref[...] += jnp.sum(chosen, axis=0, keepdims=True)


def _router(x, g, whi, wlo, bias):
    T, D = x.shape
    tm = ROW_TILE
    const = lambda i: (0, 0)
    row = lambda wd: pl.BlockSpec((tm, wd), lambda i: (i, 0))
    return pl.pallas_call(
        _router_kernel,
        grid=(T // tm,),
        in_specs=[row(D), pl.BlockSpec((1, D), const), pl.BlockSpec((D, ROUTER_COLS), const),
                  pl.BlockSpec((D, ROUTER_COLS), const), pl.BlockSpec((1, ROUTER_COLS), const)],
        out_specs=[row(D), row(ROUTER_COLS), row(ROUTER_COLS), pl.BlockSpec((1, ROUTER_COLS), const)],
        out_shape=[jax.ShapeDtypeStruct((T, D), F32), jax.ShapeDtypeStruct((T, ROUTER_COLS), F32),
                   jax.ShapeDtypeStruct((T, ROUTER_COLS), jnp.int32),
                   jax.ShapeDtypeStruct((1, ROUTER_COLS), F32)],
        compiler_params=_cparams(("arbitrary",)),
        name="router",
    )(x, g, whi, wlo, bias)


def _slot_kernel(eid_ref, pstart_ref, slot_ref, run_ref):
    tm = eid_ref.shape[0]

    @pl.when(pl.program_id(0) == 0)
    def _():
        run_ref[...] = jnp.zeros_like(run_ref)

    eid = eid_ref[...]
    lane = _lane_iota(eid.shape)
    oh0 = lane == (eid[:, 0:1] + ROUTER_E0)
    oh1 = lane == (eid[:, 1:2] + ROUTER_E0)
    both = (oh0 | oh1).astype(BF16)
    r = lax.broadcasted_iota(jnp.int32, (tm, tm), 0)
    c = lax.broadcasted_iota(jnp.int32, (tm, tm), 1)
    earlier = (c < r).astype(BF16)
    before = jnp.dot(earlier, both, preferred_element_type=F32)
    base = pstart_ref[...] + run_ref[...] + before
    s0 = jnp.sum(jnp.where(oh0, base, 0.0), axis=-1, keepdims=True)
    s1 = jnp.sum(jnp.where(oh1, base, 0.0), axis=-1, keepdims=True)
    slot_ref[...] = jnp.where(lane == 0, s0, jnp.where(lane == 1, s1, 0.0)).astype(jnp.int32)
    run_ref[...] += jnp.sum(both.astype(F32), axis=0, keepdims=True)


def _slots(eid, pstart_row):
    T = eid.shape[0]
    tm = ROW_TILE
    return pl.pallas_call(
        _slot_kernel,
        grid=(T // tm,),
        in_specs=[pl.BlockSpec((tm, ROUTER_COLS), lambda i: (i, 0)),
                  pl.BlockSpec((1, ROUTER_COLS), lambda i: (0, 0))],
        out_specs=pl.BlockSpec((tm, ROUTER_COLS), lambda i: (i, 0)),
        out_shape=jax.ShapeDtypeStruct((T, ROUTER_COLS), jnp.int32),
        scratch_shapes=[pltpu.VMEM((1, ROUTER_COLS), F32)],
        compiler_params=_cparams(("arbitrary",)),
        name="slots",
    )(eid, pstart_row)


def _expert_kernel(te_ref, nt_ref, xs_ref, wg_ref, wu_ref, wd_ref, y_ref, wg_s, wu_s, wd_s):
    t = pl.program_id(0)
    live = t < nt_ref[0]
    new_expert = (t == 0) | (te_ref[t] != te_ref[jnp.maximum(t - 1, 0)])

    @pl.when(live & new_expert)
    def _():
        wg_s[...] = wg_ref[0].astype(BF16)
        wu_s[...] = wu_ref[0].astype(BF16)
        wd_s[...] = wd_ref[0].astype(BF16)

    @pl.when(live)
    def _():
        xs = xs_ref[...].astype(BF16)
        a = jnp.dot(xs, wg_s[...], preferred_element_type=F32)
        u = jnp.dot(xs, wu_s[...], preferred_element_type=F32)
        h = (a * jax.nn.sigmoid(a) * u).astype(BF16)
        y_ref[...] = jnp.dot(h, wd_s[...], preferred_element_type=F32)

    @pl.when(jnp.logical_not(live))
    def _():
        y_ref[...] = jnp.zeros_like(y_ref)


def _experts(tile_expert, n_tiles, xs, wg, wu, wd, layer):
    Mp, D = xs.shape
    te = EXPERT_TILE
    grid_spec = pltpu.PrefetchScalarGridSpec(
        num_scalar_prefetch=2,
        grid=(Mp // te,),
        in_specs=[
            pl.BlockSpec((te, D), lambda t, te_r, nt_r: (t, 0)),
            pl.BlockSpec((None, 1, D, D_EXPERT), lambda t, te_r, nt_r: (layer, te_r[t], 0, 0)),
            pl.BlockSpec((None, 1, D, D_EXPERT), lambda t, te_r, nt_r: (layer, te_r[t], 0, 0)),
            pl.BlockSpec((None, 1, D_EXPERT, D), lambda t, te_r, nt_r: (layer, te_r[t], 0, 0)),
        ],
        out_specs=pl.BlockSpec((te, D), lambda t, te_r, nt_r: (t, 0)),
        scratch_shapes=[pltpu.VMEM((D, D_EXPERT), BF16), pltpu.VMEM((D, D_EXPERT), BF16),
                        pltpu.VMEM((D_EXPERT, D), BF16)],
    )
    return pl.pallas_call(
        _expert_kernel,
        grid_spec=grid_spec,
        out_shape=jax.ShapeDtypeStruct((Mp, D), F32),
        compiler_params=_cparams(("arbitrary",)),
        name="experts",
    )(tile_expert, n_tiles, xs, wg, wu, wd)


def _combine_kernel(x_ref, y0_ref, y1_ref, gate_ref, o_ref):
    gate = gate_ref[...]
    o_ref[...] = x_ref[...] + gate[:, 0:1] * y0_ref[...] + gate[:, 1:2] * y1_ref[...]


def _combine(x, y0, y1, gate):
    T, D = x.shape
    tm = ROW_TILE
    row = lambda wd: pl.BlockSpec((tm, wd), lambda i: (i, 0))
    return pl.pallas_call(
        _combine_kernel,
        grid=(T // tm,),
        in_specs=[row(D), row(D), row(D), row(ROUTER_COLS)],
        out_specs=row(D),
        out_shape=jax.ShapeDtypeStruct((T, D), F32),
        compiler_params=_cparams(("parallel",)),
        name="moe_combine",
    )(x, y0, y1, gate)


def _moe(x, p, layer):
    T, D = x.shape
    xn, gate, eid, counts_row = _router(x, p["ffn_norm"], p["r_whi"], p["r_wlo"], p["r_bias"])
    M = T * TOP_K
    te = EXPERT_TILE
    Mp = M + N_EXPERTS * te
    counts = counts_row[0, ROUTER_E0:ROUTER_E0 + N_EXPERTS].astype(jnp.int32)
    padded = ((counts + te - 1) // te) * te
    pend = jnp.cumsum(padded)
    pstart = pend - padded
    pstart_row = jnp.zeros((1, ROUTER_COLS), F32).at[0, ROUTER_E0:ROUTER_E0 + N_EXPERTS].set(pstart.astype(F32))
    slot = _slots(eid, pstart_row)[:, :TOP_K]
    tok_of_slot = jnp.zeros((Mp,), jnp.int32).at[slot.reshape(M)].set(
        jnp.arange(M, dtype=jnp.int32) // TOP_K, mode="promise_in_bounds", unique_indices=True)
    tile_ids = jnp.arange(Mp // te, dtype=jnp.int32)
    tile_expert = jnp.minimum(jnp.sum((pend // te)[None, :] <= tile_ids[:, None], axis=1),
                              N_EXPERTS - 1).astype(jnp.int32)
    n_tiles = (pend[-1] // te).astype(jnp.int32).reshape(1)
    rows = lambda a, idx: a.at[idx].get(mode="promise_in_bounds")
    xs = rows(xn, tok_of_slot)
    ys = _experts(tile_expert, n_tiles, xs, p["w_gate"], p["w_up"], p["w_down"], layer)
    return _combine(x, rows(ys, slot[:, 0]), rows(ys, slot[:, 1]), gate)


def _rope_tables(seq):
    half = HEAD_DIM // 2
    inv = ROPE_THETA ** (-jnp.arange(half, dtype=F32) / half)
    ang = jnp.arange(seq, dtype=F32)[:, None] * inv[None, :]
    cos = jnp.tile(jnp.cos(ang), (1, LANES // half))
    sin = jnp.sin(ang)
    sin = jnp.tile(jnp.concatenate([-sin, sin], axis=1), (1, LANES // HEAD_DIM))
    return cos, sin


def _head_gain(g, heads):
    return jnp.tile(g.astype(F32), (heads,))


def _prepare(ev_norm, ev_w_in, ev_qn_a, ev_kn_a, ev_qn_b, ev_kn_b, ev_sink, ev_w_out,
             od_norm, od_w_in, od_qn, od_kn, od_lq1, od_lk1, od_lq2, od_lk2, od_subln, od_w_out,
             ffn_norm, rg_w, rg_b, re_w, re_b, w_gate, w_up, w_down):
    hd = HEAD_DIM
    P = {}
    r = jnp.arange(MXU_DIM) // hd
    P["bd"] = (r[:, None] == r[None, :]).astype(BF16)
    w = ev_w_in[0]
    a3 = 3 * A_HEADS * hd
    qb_end = a3 + B_Q_HEADS * hd
    kb = [w[:, qb_end + h * hd:qb_end + (h + 1) * hd] for h in range(B_KV_HEADS)]
    vb0 = qb_end + B_KV_HEADS * hd
    vb = [w[:, vb0 + h * hd:vb0 + (h + 1) * hd] for h in range(B_KV_HEADS)]
    P["ev_w_in"] = jnp.concatenate([w[:, :qb_end]] + [c for h in kb for c in (h, h)]
                                   + [c for h in vb for c in (h, h)], axis=1).astype(BF16)
    ones = jnp.ones((A_HEADS * hd,), F32)
    P["ev_hg"] = jnp.concatenate([
        _head_gain(ev_qn_a[0], A_HEADS), _head_gain(ev_kn_a[0], A_HEADS), ones,
        _head_gain(ev_qn_b[0], B_Q_HEADS), _head_gain(ev_kn_b[0], 2 * B_KV_HEADS),
        jnp.ones((2 * B_KV_HEADS * hd,), F32)])[None, :]
    P["ev_norm"] = ev_norm[0][None, :]
    P["ev_sink"] = ev_sink[0].astype(F32)
    P["ev_w_out"] = ev_w_out[0].astype(BF16)
    P["od_w_in"] = od_w_in[0].astype(BF16)
    P["od_hg"] = jnp.concatenate([_head_gain(od_qn[0], 2 * C_HEADS), _head_gain(od_kn[0], 2 * C_HEADS),
                                  jnp.ones((2 * C_HEADS * hd,), F32)])[None, :]
    P["od_norm"] = od_norm[0][None, :]
    P["od_lam"] = jnp.stack([od_lq1[0], od_lk1[0], od_lq2[0], od_lk2[0]]).astype(F32)
    P["od_subln"] = od_subln[0][:, None].astype(F32)
    P["od_w_out"] = od_w_out[0].astype(BF16)
    P["moe"] = []
    for layer in range(ffn_norm.shape[0]):
        wr = jnp.zeros((D_MODEL, ROUTER_COLS), F32)
        wr = wr.at[:, :N_GROUPS].set(rg_w[layer])
        we = jnp.transpose(re_w[layer], (1, 0, 2)).reshape(D_MODEL, N_EXPERTS)
        wr = wr.at[:, ROUTER_E0:ROUTER_E0 + N_EXPERTS].set(we)
        bias = jnp.zeros((ROUTER_COLS,), F32).at[:N_GROUPS].set(rg_b[layer])
        bias = bias.at[ROUTER_E0:ROUTER_E0 + N_EXPERTS].set(re_b[layer].reshape(N_EXPERTS))
        whi = wr.astype(BF16)
        P["moe"].append({
            "ffn_norm": ffn_norm[layer][None, :],
            "r_whi": whi, "r_wlo": (wr - whi.astype(F32)).astype(BF16), "r_bias": bias[None, :],
            "w_gate": w_gate, "w_up": w_up, "w_down": w_down,
        })
    return P


def _even_mixer(x, P, B, S):
    T = B * S
    hd = HEAD_DIM
    aw = A_HEADS * hd
    cos_t, sin_t = _rope_tables(S)
    scale = hd ** -0.5
    tm = ROW_TILE
    dils = [d for _, d in DILATION_GROUPS]
    assert dils[0] == 1
    out_blocks = [(tm, aw)] * 4 + [(tm, 2 * LANES)] * 2
    views = []
    for _ in range(3):
        tv = []
        for d in dils[1:]:
            tv.append((d, len(out_blocks)))
            out_blocks.append((tm // d, d * aw))
        views.append(tuple(tv))
    plan = ((0, 0, aw, True, scale, views[0]), (1, aw, aw, True, 1.0, views[1]),
            (2, 2 * aw, aw, False, 1.0, views[2]), (3, 3 * aw, aw, True, scale, ()),
            (4, 4 * aw, 2 * LANES, True, 1.0, ()), (5, 4 * aw + 2 * LANES, 2 * LANES, False, 1.0, ()))
    res = _inproj(x, P["ev_norm"], P["ev_w_in"], P["ev_hg"], cos_t, sin_t, P["bd"], plan, out_blocks, S,
                  stage_width=aw)
    qb, kb, vb = res[3:6]
    outs, lses = [], []
    for gi, (window, dil) in enumerate(DILATION_GROUPS):
        L = S // dil
        qkv = [res[t] if dil == 1 else res[views[t][gi - 1][1]] for t in range(3)]
        o, lse = _band_a(*[t.reshape(B, L, dil * aw) for t in qkv], window // (2 * dil))
        outs.append(o.reshape(T // dil, dil * aw))
        lses.append(lse.reshape(T // dil, dil * aw))
    ob = _band_b(P["ev_sink"], qb.reshape(B, S, aw), kb.reshape(B, S, 2 * LANES), vb.reshape(B, S, 2 * LANES))
    return _outproj_even(outs[0], outs[1], outs[2], lses[0], lses[1], lses[2], ob.reshape(T, aw),
                         P["ev_w_out"], x)


def _diff_mixer(x, P, B, S, lam_init):
    T = B * S
    hd = HEAD_DIM
    cw = 2 * C_HEADS * hd
    cos_t, sin_t = _rope_tables(S)
    plan = ((0, 0, cw, True, LOG2_E * hd ** -0.5, ()), (1, cw, cw, True, 1.0, ()), (2, 2 * cw, cw, False, 1.0, ()))
    q, k, v = _inproj(x, P["od_norm"], P["od_w_in"], P["od_hg"], cos_t, sin_t, P["bd"], plan,
                      [(ROW_TILE, cw)] * 3, S)
    tk = DIFF_TK
    qT = q.reshape(B, S, C_HEADS, LANES).transpose(0, 2, 3, 1)
    vT = v.reshape(B, S // tk, tk, C_HEADS, LANES).transpose(0, 3, 1, 4, 2)
    o = _diff_attention(P["od_lam"], qT, k.reshape(B, S, cw), vT, P["od_subln"], lam_init)
    return _outproj(o.reshape(T, cw), P["od_w_out"], x)


def _trunk(x3, P):
    B, S, D = x3.shape
    x = x3.reshape(B * S, D)
    x = _even_mixer(x, P, B, S)
    x = _moe(x, P["moe"][0], 0)
    lam_init = 0.8 - 0.6 * math.exp(-0.3 * 1)
    x = _diff_mixer(x, P, B, S, lam_init)
    x = _moe(x, P["moe"][1], 1)
    return x.reshape(B, S, D)


def kernel(x_prompt, x_sample, ev_norm, ev_w_in, ev_qn_a, ev_kn_a, ev_qn_b, ev_kn_b, ev_sink, ev_w_out,
           od_norm, od_w_in, od_qn, od_kn, od_lq1, od_lk1, od_lq2, od_lk2, od_subln, od_w_out,
           ffn_norm, rg_w, rg_b, re_w, re_b, w_gate, w_up, w_down):
    P = _prepare(ev_norm, ev_w_in, ev_qn_a, ev_kn_a, ev_qn_b, ev_kn_b, ev_sink, ev_w_out,
                 od_norm, od_w_in, od_qn, od_kn, od_lq1, od_lk1, od_lq2, od_lk2, od_subln, od_w_out,
                 ffn_norm, rg_w, rg_b, re_w, re_b, w_gate, w_up, w_down)
    return (_trunk(x_prompt, P), _trunk(x_sample, P))
```

```python
import functools
import math

import jax
import jax.numpy as jnp
from jax import lax
from jax.experimental import pallas as pl
from jax.experimental.pallas import tpu as pltpu

F32 = jnp.float32
BF16 = jnp.bfloat16

D_MODEL = 1024
HEAD_DIM = 64
A_HEADS = 8
DILATION_GROUPS = ((128, 1), (512, 4), (2048, 16))
B_Q_HEADS = 8
B_KV_HEADS = 2
B_HALF_WINDOW = 128
C_HEADS = 8
N_GROUPS = 4
EXPERTS_PER_GROUP = 8
N_EXPERTS = N_GROUPS * EXPERTS_PER_GROUP
TOP_K = 2
D_EXPERT = 512
ROPE_THETA = 10000.0
EPS = 1e-6
NEG_INF = -1e30

LANES = 128
MXU_DIM = 256
VMEM_LIMIT_BYTES = 52 * 1024 * 1024

ROW_TILE = 512
COL_CHUNK = 256
BAND_A_TQ = 128
BAND_TQ = 256
BAND_KBLK = 128
BAND_ONES_ROWS = 16
BAND_UNROLL = 2
BAND_BLOCK_BYTES =2 * 1024 * 1024
DIFF_TQ = 512
DIFF_TK = 256
DIFF_UNROLL = 8
DIFF_ONES_ROWS = 16
LOG2_E = 1.4426950408889634
EXPERT_TILE = 256
ROUTER_COLS = 128
ROUTER_E0 = 8
DISPATCH_TABLE_ROWS = 32768


def _cparams(sem):
    return pltpu.CompilerParams(dimension_semantics=sem, vmem_limit_bytes=VMEM_LIMIT_BYTES)


def _lane_iota(shape):
    return lax.broadcasted_iota(jnp.int32, shape, len(shape) - 1)


def _inproj_kernel(x_ref, g_ref, w_ref, hg_ref, cos_ref, sin_ref, bd_ref, *refs, plan, n_out):
    out_refs = refs[:n_out]
    stage_ref = refs[n_out] if len(refs) > n_out else None
    x = x_ref[...]
    tm = x.shape[0]
    ms = jnp.mean(x * x, axis=-1, keepdims=True)
    xn = (x * lax.rsqrt(ms + EPS) * g_ref[...]).astype(BF16)
    lane = _lane_iota((1, LANES))
    first_half = (lane % HEAD_DIM) < (HEAD_DIM // 2)
    cos = cos_ref[...]
    sin = sin_ref[...]
    for oi, c0, width, normed, scale, views in plan:
        for cc in range(0, width, COL_CHUNK):
            cw = min(COL_CHUNK, width - cc)
            y = jnp.dot(xn, w_ref[:, c0 + cc:c0 + cc + cw], preferred_element_type=F32)
            if normed:
                hs = jnp.dot((y * y).astype(BF16), bd_ref[:cw, :cw], preferred_element_type=F32)
                y = y * lax.rsqrt(hs * (1.0 / HEAD_DIM) + EPS) * hg_ref[:, c0 + cc:c0 + cc + cw]
            for j in range(cw // LANES):
                r = y[:, j * LANES:(j + 1) * LANES]
                if normed:
                    rot = jnp.where(first_half, pltpu.roll(r, LANES - HEAD_DIM // 2, 1),
                                    pltpu.roll(r, HEAD_DIM // 2, 1))
                    r = r * cos + rot * sin
                    if scale != 1.0:
                        r = r * scale
                cols = slice(cc + j * LANES, cc + (j + 1) * LANES)
                out_refs[oi][:, cols] = r.astype(BF16)
                if views:
                    stage_ref[cc // LANES + j] = r
        for d, vi in views:
            for rr in range(d):
                for j in range(width // LANES):
                    out_refs[vi][:, rr * width + j * LANES:rr * width + (j + 1) * LANES] = (
                        stage_ref[j, pl.ds(rr, tm // d, stride=d), :].astype(BF16))


def _inproj(x, g, w, hg, cos_t, sin_t, bd, plan, out_blocks, seq, stage_width=0):
    T, D = x.shape
    N = w.shape[1]
    tm = ROW_TILE
    n_pos_blocks = seq // tm
    const = lambda i: (0, 0)
    scratch = [pltpu.VMEM((stage_width // LANES, tm, LANES), F32)] if stage_width else []
    return pl.pallas_call(
        functools.partial(_inproj_kernel, plan=plan, n_out=len(out_blocks)),
        grid=(T // tm,),
        in_specs=[
            pl.BlockSpec((tm, D), lambda i: (i, 0)),
            pl.BlockSpec((1, D), const),
            pl.BlockSpec((D, N), const),
            pl.BlockSpec((1, N), const),
            pl.BlockSpec((tm, LANES), lambda i: (i % n_pos_blocks, 0)),
            pl.BlockSpec((tm, LANES), lambda i: (i % n_pos_blocks, 0)),
            pl.BlockSpec((MXU_DIM, MXU_DIM), const),
        ],
        out_specs=[pl.BlockSpec((br, bc), lambda i: (i, 0)) for br, bc in out_blocks],
        out_shape=[jax.ShapeDtypeStruct((T // tm * br, bc), BF16) for br, bc in out_blocks],
        scratch_shapes=scratch,
        compiler_params=_cparams(("parallel",)),
        name="inproj",
    )(x, g, w, hg, cos_t, sin_t, bd)


def _band_window(i, tq, hw, L):
    win = tq + 2 * hw
    start = jnp.clip(i * tq - hw, 0, L - win)
    start = pl.multiple_of(start, 64)
    qpos = i * tq + lax.broadcasted_iota(jnp.int32, (tq, win), 0)
    kpos = start + lax.broadcasted_iota(jnp.int32, (tq, win), 1)
    valid = jnp.abs(kpos - qpos) <= hw
    return start, win, valid


def _band_a_kernel(q_ref, k_ref, v_ref, o_ref, lse_ref, *, hw, tq, L, wb):
    lane = _lane_iota((1, LANES))
    head0 = lane < HEAD_DIM

    def qblock(i, carry):
        start, win, valid = _band_window(i, tq, hw, L)
        rows = pl.ds(pl.multiple_of(i * tq, tq), tq)
        for sl in range(wb // LANES):
            cols = slice(sl * LANES, (sl + 1) * LANES)
            q = q_ref[0, rows, cols]
            kw = k_ref[0, pl.ds(start, win), cols]
            vw = v_ref[0, pl.ds(start, win), cols]
            outs = []
            lses = []
            for hmask in (head0, jnp.logical_not(head0)):
                qh = jnp.where(hmask, q, jnp.zeros_like(q))
                s = lax.dot_general(qh, kw, (((1,), (1,)), ((), ())), preferred_element_type=F32)
                s = jnp.where(valid, s, NEG_INF)
                m = jnp.max(s, axis=-1, keepdims=True)
                p = jnp.exp(s - m)
                den = jnp.sum(p, axis=-1, keepdims=True)
                pv = jnp.dot(p.astype(BF16), vw, preferred_element_type=F32)
                outs.append(pv / den)
                lses.append(m + jnp.log(den))
            o_ref[0, rows, cols] = jnp.where(head0, outs[0], outs[1]).astype(o_ref.dtype)
            lse_ref[0, rows, cols] = jnp.where(head0, lses[0], lses[1])
        return carry

    lax.fori_loop(0, L // tq, qblock, 0, unroll=BAND_UNROLL)


def _band_a(q, k, v, hw):
    NB, L, C = q.shape
    tq = BAND_A_TQ
    assert L >= tq + 2 * hw and L % tq == 0
    wb = min(C, A_HEADS * HEAD_DIM)
    while L * wb * 2 > BAND_BLOCK_BYTES and wb > LANES:
        wb //= 2
    spec = pl.BlockSpec((1, L, wb), lambda n, c: (n, 0, c))
    return pl.pallas_call(
        functools.partial(_band_a_kernel, hw=hw, tq=tq, L=L, wb=wb),
        grid=(NB, C // wb),
        in_specs=[spec, spec, spec],
        out_specs=[spec, spec],
        out_shape=[jax.ShapeDtypeStruct((NB, L, C), BF16), jax.ShapeDtypeStruct((NB, L, C), F32)],
        compiler_params=_cparams(("parallel", "parallel")),
        name="band_a",
    )(q, k, v)


def _band_core(i, L, hw, q_heads, k_ref, kcols, vT_scr, vslab, sink_row):
    tq = min(BAND_TQ, L)
    kb = BAND_KBLK
    nblk = L // kb
    wblk = min(tq // kb + 2, nblk)
    win = wblk * kb
    blk0 = jnp.clip(i * (tq // kb) - 1, 0, nblk - wblk)
    kw = k_ref[0, pl.ds(pl.multiple_of(blk0 * kb, kb), win), kcols]
    qq = jnp.concatenate(q_heads, axis=0)
    s = lax.dot_general(kw, qq, (((1,), (1,)), ((), ())), preferred_element_type=F32)
    kpos = blk0 * kb + lax.broadcasted_iota(jnp.int32, (win, tq), 0)
    qpos = i * tq + lax.broadcasted_iota(jnp.int32, (win, tq), 1)
    valid = jnp.abs(kpos - qpos) <= hw
    s = jnp.where(jnp.concatenate([valid] * len(q_heads), axis=1), s, NEG_INF)
    m = jnp.max(s, axis=0, keepdims=True)
    if sink_row is not None:
        m = jnp.maximum(m, sink_row)
    p = jnp.exp(s - m).astype(BF16)
    vT = jnp.concatenate([vT_scr[vslab, blk0 + w] for w in range(wblk)], axis=1)
    lhs = jnp.concatenate([vT, jnp.ones((BAND_ONES_ROWS, win), BF16)], axis=0)
    return jnp.dot(lhs, p, preferred_element_type=F32), m


def _transpose_v(v_ref, vT_scr, L, n_slabs):
    def blk_step(blk, carry):
        rows = pl.ds(pl.multiple_of(blk * BAND_KBLK, BAND_KBLK), BAND_KBLK)
        for sl in range(n_slabs):
            vb = v_ref[0, rows, sl * LANES:(sl + 1) * LANES].astype(F32)
            vT_scr[sl, blk] = vb.T.astype(BF16)
        return carry
    lax.fori_loop(0, L // BAND_KBLK, blk_step, 0)


def _band_b_kernel(sink_ref, q_ref, k_ref, v_ref, o_ref, vT_scr, *, hw, L):
    tq = min(BAND_TQ, L)
    lane = _lane_iota((1, LANES))
    head0 = lane < HEAD_DIM
    row0 = lax.broadcasted_iota(jnp.int32, (LANES, 1), 0) < HEAD_DIM
    hk = pl.program_id(1)
    group = B_Q_HEADS // B_KV_HEADS
    sink = jnp.concatenate(
        [jnp.full((1, tq), sink_ref[hk * group + g], F32) for g in range(group)], axis=1)
    _transpose_v(v_ref, vT_scr, L, 1)

    def qblock(i, carry):
        rows = pl.ds(pl.multiple_of(i * tq, tq), tq)
        qs = []
        for g in range(group):
            q = q_ref[0, rows, (g // 2) * LANES:(g // 2 + 1) * LANES]
            zero = jnp.zeros_like(q)
            qs.append(jnp.where(head0, q, zero) if g % 2 == 0 else jnp.where(head0, zero, q))
        acc, m = _band_core(i, L, hw, qs, k_ref, slice(0, LANES), vT_scr, 0, sink)
        den = acc[LANES:LANES + 1] + jnp.exp(sink - m)
        oT = acc[:LANES] / den
        for pair in range(group // 2):
            a = oT[:, (2 * pair) * tq:(2 * pair + 1) * tq]
            b = oT[:, (2 * pair + 1) * tq:(2 * pair + 2) * tq]
            o_ref[0, rows, pair * LANES:(pair + 1) * LANES] = jnp.where(row0, a, b).T.astype(o_ref.dtype)
        return carry

    lax.fori_loop(0, L // tq, qblock, 0, unroll=min(BAND_UNROLL, L // tq))


def _band_b(sink, q, kdup, vdup):
    B, S, C = q.shape
    tq = BAND_TQ
    hw = B_HALF_WINDOW
    qw = C // B_KV_HEADS
    assert hw <= BAND_KBLK and S % tq == 0
    return pl.pallas_call(
        functools.partial(_band_b_kernel, hw=hw, L=S),
        grid=(B, B_KV_HEADS),
        in_specs=[
            pl.BlockSpec(memory_space=pltpu.SMEM),
            pl.BlockSpec((1, S, qw), lambda b, h: (b, 0, h)),
            pl.BlockSpec((1, S, LANES), lambda b, h: (b, 0, h)),
            pl.BlockSpec((1, S, LANES), lambda b, h: (b, 0, h)),
        ],
        out_specs=pl.BlockSpec((1, S, qw), lambda b, h: (b, 0, h)),
        out_shape=jax.ShapeDtypeStruct((B, S, C), BF16),
        scratch_shapes=[pltpu.VMEM((1, S // BAND_KBLK, LANES, BAND_KBLK), BF16)],
        compiler_params=_cparams(("parallel", "parallel")),
        name="band_b",
    )(sink, q, kdup, vdup)


def _outproj_even_kernel(o1_ref, o4_ref, o16_ref, l1_ref, l4_ref, l16_ref, ob_ref, w_ref, x_ref, y_ref,
                         so4, sl4, so16, sl16):
    tm, half = o1_ref.shape

    def token_major(src_ref, stage_ref, d):
        for rr in range(d):
            for j in range(half // LANES):
                stage_ref[j, pl.ds(rr, tm // d, stride=d), :] = (
                    src_ref[:, rr * half + j * LANES:rr * half + (j + 1) * LANES].astype(F32))
        return jnp.concatenate([stage_ref[j] for j in range(half // LANES)], axis=1)

    dils = [d for _, d in DILATION_GROUPS]
    l1 = l1_ref[...]
    o4, l4 = token_major(o4_ref, so4, dils[1]), token_major(l4_ref, sl4, dils[1])
    o16, l16 = token_major(o16_ref, so16, dils[2]), token_major(l16_ref, sl16, dils[2])
    m = jnp.maximum(jnp.maximum(l1, l4), l16)
    e1, e4, e16 = jnp.exp(l1 - m), jnp.exp(l4 - m), jnp.exp(l16 - m)
    num = e1 * o1_ref[...].astype(F32) + e4 * o4 + e16 * o16
    oa = (num / (e1 + e4 + e16)).astype(BF16)
    ka = oa.shape[1]
    acc = jnp.dot(oa, w_ref[:ka, :], preferred_element_type=F32)
    acc = acc + jnp.dot(ob_ref[...], w_ref[ka:, :], preferred_element_type=F32)
    y_ref[...] = x_ref[...] + acc


def _outproj_even(o1, o4, o16, l1, l4, l16, ob, w, x):
    T, D = x.shape
    tm = ROW_TILE
    half = o1.shape[1]
    row = lambda wd: pl.BlockSpec((tm, wd), lambda i: (i, 0))
    view = lambda d: pl.BlockSpec((tm // d, d * half), lambda i: (i, 0))
    d4, d16 = DILATION_GROUPS[1][1], DILATION_GROUPS[2][1]
    return pl.pallas_call(
        _outproj_even_kernel,
        grid=(T // tm,),
        in_specs=[row(half), view(d4), view(d16), row(half), view(d4), view(d16), row(half),
                  pl.BlockSpec(w.shape, lambda i: (0, 0)), row(D)],
        out_specs=row(D),
        out_shape=jax.ShapeDtypeStruct((T, D), F32),
        scratch_shapes=[pltpu.VMEM((half // LANES, tm, LANES), F32)] * 4,
        compiler_params=_cparams(("parallel",)),
        name="outproj_even",
    )(o1, o4, o16, l1, l4, l16, ob, w, x)


def _outproj_kernel(o_ref, w_ref, x_ref, y_ref):
    y_ref[...] = x_ref[...] + jnp.dot(o_ref[...], w_ref[...], preferred_element_type=F32)


def _outproj(o, w, x):
    T, D = x.shape
    tm = ROW_TILE
    row = lambda wd: pl.BlockSpec((tm, wd), lambda i: (i, 0))
    return pl.pallas_call(
        _outproj_kernel,
        grid=(T // tm,),
        in_specs=[row(o.shape[1]), pl.BlockSpec(w.shape, lambda i: (0, 0)), row(D)],
        out_specs=row(D),
        out_shape=jax.ShapeDtypeStruct((T, D), F32),
        compiler_params=_cparams(("parallel",)),
        name="outproj",
    )(o, w, x)


def _diff_kernel(lam_ref, qT_ref, k_ref, vT_ref, sub_ref, o_ref, s0_scr, s1_scr, acc_scr, *,
                 tq, tk, S, lam_init, unroll):
    row = lax.broadcasted_iota(jnp.int32, (LANES, 1), 0)
    comp0 = row < HEAD_DIM
    lamv = lam_ref[...]
    lam = (jnp.exp(jnp.sum(lamv[0:1] * lamv[1:2], axis=-1, keepdims=True))
           - jnp.exp(jnp.sum(lamv[2:3] * lamv[3:4], axis=-1, keepdims=True)) + lam_init)
    qT = qT_ref[0, 0]
    zero = jnp.zeros_like(qT)
    qqT = jnp.concatenate([jnp.where(comp0, qT, zero), jnp.where(comp0, zero, qT)], axis=1)
    ones = jnp.ones((DIFF_ONES_ROWS, tk), BF16)
    n_kv = S // tk

    def scores(t, dst):
        kb = k_ref[0, pl.ds(pl.multiple_of(t * tk, tk), tk), :]
        dst[...] = jnp.dot(kb, qqT, preferred_element_type=F32)

    def softmax_pv(src, t, m):
        s = src[...]
        lhs = jnp.concatenate([vT_ref[0, 0, t], ones], axis=0)
        m_new = jnp.maximum(m, jnp.max(s, axis=0, keepdims=True))
        alpha = jnp.exp2(m - m_new)
        p = jnp.exp2(s - m_new).astype(BF16)
        acc_scr[...] = alpha * acc_scr[...] + jnp.dot(lhs, p, preferred_element_type=F32)
        return m_new

    def step(i, m):
        bufs = (s0_scr, s1_scr)
        for u in range(unroll):
            t = unroll * i + u
            scores(jnp.minimum(t + 1, n_kv - 1), bufs[(u + 1) % 2])
            m = softmax_pv(bufs[u % 2], t, m)
        return m

    acc_scr[...] = jnp.zeros_like(acc_scr)
    scores(0, s0_scr)
    lax.fori_loop(0, n_kv // unroll, step, jnp.full((1, 2 * tq), NEG_INF, F32))
    acc = acc_scr[...]
    a = acc[:LANES] / acc[LANES:LANES + 1]
    o = a[:, :tq] - lam * a[:, tq:]
    ms = jnp.mean(o * o, axis=0, keepdims=True)
    o = o * lax.rsqrt(ms + EPS) * sub_ref[...] * (1.0 - lam_init)
    o_ref[0] = o.T.astype(o_ref.dtype)


def _diff_attention(lam_vecs, qT, k, vT, subln_col, lam_init):
    B, H, _, S = qT.shape
    tq, tk = DIFF_TQ, DIFF_TK
    unroll = min(DIFF_UNROLL, S // tk)
    assert S % tq == 0 and S % (tk * unroll) == 0 and unroll % 2 == 0
    acc_rows = LANES + DIFF_ONES_ROWS
    return pl.pallas_call(
        functools.partial(_diff_kernel, tq=tq, tk=tk, S=S, lam_init=lam_init, unroll=unroll),
        grid=(B, H, S // tq),
        in_specs=[
            pl.BlockSpec(lam_vecs.shape, lambda b, h, i: (0, 0)),
            pl.BlockSpec((1, 1, LANES, tq), lambda b, h, i: (b, h, 0, i)),
            pl.BlockSpec((1, S, LANES), lambda b, h, i: (b, 0, h)),
            pl.BlockSpec((1, 1, S // tk, LANES, tk), lambda b, h, i: (b, h, 0, 0, 0)),
            pl.BlockSpec((LANES, 1), lambda b, h, i: (0, 0)),
        ],
        out_specs=pl.BlockSpec((1, tq, LANES), lambda b, h, i: (b, i, h)),
        out_shape=jax.ShapeDtypeStruct((B, S, H * LANES), BF16),
        scratch_shapes=[pltpu.VMEM((tk, 2 * tq), F32), pltpu.VMEM((tk, 2 * tq), F32),
                        pltpu.VMEM((acc_rows, 2 * tq), F32)],
        compiler_params=_cparams(("parallel", "parallel", "arbitrary")),
        name="diff_attn",
    )(lam_vecs, qT, k, vT, subln_col)


def _router_kernel(x_ref, g_ref, whi_ref, wlo_ref, b_ref, xn_ref, gate_ref, eid_ref, cnt_ref):
    x = x_ref[...]
    ms = jnp.mean(x * x, axis=-1, keepdims=True)
    xn = x * lax.rsqrt(ms + EPS) * g_ref[...]
    xhi = xn.astype(BF16)
    half = xn.shape[1] // 2
    hi_bits = lax.bitcast_convert_type(xhi[:, :half].astype(F32), jnp.uint32)
    lo_bits = lax.bitcast_convert_type(xhi[:, half:].astype(F32), jnp.uint32)
    xn_ref[...] = hi_bits | (lo_bits >> 16)
    xlo = (xn - xhi.astype(F32)).astype(BF16)
    whi = whi_ref[...]
    logits = (jnp.dot(xhi, whi, preferred_element_type=F32)
              + jnp.dot(xlo, whi, preferred_element_type=F32)
              + jnp.dot(xhi, wlo_ref[...], preferred_element_type=F32)) + b_ref[...]
    lane = _lane_iota(logits.shape).astype(F32)
    big = float(4 * LANES)
    lg = jnp.where(lane < N_GROUPS, logits, NEG_INF)
    mg = jnp.max(lg, axis=-1, keepdims=True)
    g_sel = jnp.min(jnp.where(lg == mg, lane, big), axis=-1, keepdims=True)
    p_sel = 1.0 / jnp.sum(jnp.exp(lg - mg), axis=-1, keepdims=True)
    e_lo = ROUTER_E0 + g_sel * EXPERTS_PER_GROUP
    le = jnp.where((lane >= e_lo) & (lane < e_lo + EXPERTS_PER_GROUP), logits, NEG_INF)
    v1 = jnp.max(le, axis=-1, keepdims=True)
    i1 = jnp.min(jnp.where(le == v1, lane, big), axis=-1, keepdims=True)
    le2 = jnp.where(lane == i1, NEG_INF, le)
    v2 = jnp.max(le2, axis=-1, keepdims=True)
    i2 = jnp.min(jnp.where(le2 == v2, lane, big), axis=-1, keepdims=True)
    e2 = jnp.exp(v2 - v1)
    den = 1.0 + e2
    g1 = (1.0 / den) * p_sel
    g2 = (e2 / den) * p_sel
    gate_ref[...] = jnp.where(lane == 0, g1, jnp.where(lane == 1, g2, 0.0))
    eid = jnp.where(lane == 0, i1 - ROUTER_E0, jnp.where(lane == 1, i2 - ROUTER_E0, 0.0))
    eid_ref[...] = eid.astype(jnp.int32)
    chosen = ((lane == i1) | (lane == i2)).astype(F32)

    @pl.when(pl.program_id(0) == 0)
    def _():
        cnt_ref[...] = jnp.zeros_like(cnt_ref)

    cnt_ref[...] += jnp.sum(chosen, axis=0, keepdims=True)


def _router(x, g, whi, wlo, bias):
    T, D = x.shape
    tm = ROW_TILE
    const = lambda i: (0, 0)
    row = lambda wd: pl.BlockSpec((tm, wd), lambda i: (i, 0))
    return pl.pallas_call(
        _router_kernel,
        grid=(T // tm,),
        in_specs=[row(D), pl.BlockSpec((1, D), const), pl.BlockSpec((D, ROUTER_COLS), const),
                  pl.BlockSpec((D, ROUTER_COLS), const), pl.BlockSpec((1, ROUTER_COLS), const)],
        out_specs=[row(D // 2), row(ROUTER_COLS), row(ROUTER_COLS), pl.BlockSpec((1, ROUTER_COLS), const)],
        out_shape=[jax.ShapeDtypeStruct((max(T, DISPATCH_TABLE_ROWS), D // 2), jnp.uint32),
                   jax.ShapeDtypeStruct((T, ROUTER_COLS), F32),
                   jax.ShapeDtypeStruct((T, ROUTER_COLS), jnp.int32),
                   jax.ShapeDtypeStruct((1, ROUTER_COLS), F32)],
        compiler_params=_cparams(("arbitrary",)),
        name="router",
    )(x, g, whi, wlo, bias)


def _slot_kernel(eid_ref, pstart_ref, slot_ref, run_ref):
    tm = eid_ref.shape[0]

    @pl.when(pl.program_id(0) == 0)
    def _():
        run_ref[...] = jnp.zeros_like(run_ref)

    eid = eid_ref[...]
    lane = _lane_iota(eid.shape)
    oh0 = lane == (eid[:, 0:1] + ROUTER_E0)
    oh1 = lane == (eid[:, 1:2] + ROUTER_E0)
    both = (oh0 | oh1).astype(BF16)
    r = lax.broadcasted_iota(jnp.int32, (tm, tm), 0)
    c = lax.broadcasted_iota(jnp.int32, (tm, tm), 1)
    earlier = (c < r).astype(BF16)
    before = jnp.dot(earlier, both, preferred_element_type=F32)
    base = pstart_ref[...] + run_ref[...] + before
    s0 = jnp.sum(jnp.where(oh0, base, 0.0), axis=-1, keepdims=True)
    s1 = jnp.sum(jnp.where(oh1, base, 0.0), axis=-1, keepdims=True)
    slot_ref[...] = jnp.where(lane == 0, s0, jnp.where(lane == 1, s1, 0.0)).astype(jnp.int32)
    run_ref[...] += jnp.sum(both.astype(F32), axis=0, keepdims=True)


def _slots(eid, pstart_row):
    T = eid.shape[0]
    tm = ROW_TILE
    return pl.pallas_call(
        _slot_kernel,
        grid=(T // tm,),
        in_specs=[pl.BlockSpec((tm, ROUTER_COLS), lambda i: (i, 0)),
                  pl.BlockSpec((1, ROUTER_COLS), lambda i: (0, 0))],
        out_specs=pl.BlockSpec((tm, ROUTER_COLS), lambda i: (i, 0)),
        out_shape=jax.ShapeDtypeStruct((T, ROUTER_COLS), jnp.int32),
        scratch_shapes=[pltpu.VMEM((1, ROUTER_COLS), F32)],
        compiler_params=_cparams(("arbitrary",)),
        name="slots",
    )(eid, pstart_row)


def _expert_kernel(te_ref, nt_ref, xs_ref, wg_ref, wu_ref, wd_ref, y_ref, wg_s, wu_s, wd_s):
    t = pl.program_id(0)
    live = t < nt_ref[0]
    new_expert = (t == 0) | (te_ref[t] != te_ref[jnp.maximum(t - 1, 0)])

    @pl.when(live & new_expert)
    def _():
        wg_s[...] = wg_ref[0].astype(BF16)
        wu_s[...] = wu_ref[0].astype(BF16)
        wd_s[...] = wd_ref[0].astype(BF16)

    @pl.when(live)
    def _():
        words = xs_ref[...]
        hi = lax.bitcast_convert_type(words & jnp.uint32(0xFFFF0000), F32).astype(BF16)
        lo = lax.bitcast_convert_type(words << 16, F32).astype(BF16)
        xs = jnp.concatenate([hi, lo], axis=1)
        a = jnp.dot(xs, wg_s[...], preferred_element_type=F32)
        u = jnp.dot(xs, wu_s[...], preferred_element_type=F32)
        h = (a * jax.nn.sigmoid(a) * u).astype(BF16)
        y_ref[...] = jnp.dot(h, wd_s[...], preferred_element_type=F32).astype(y_ref.dtype)

    @pl.when(jnp.logical_not(live))
    def _():
        y_ref[...] = jnp.zeros_like(y_ref)


def _experts(tile_expert, n_tiles, xs, wg, wu, wd, layer):
    Mp = xs.shape[0]
    D = wg.shape[2]
    te = EXPERT_TILE
    grid_spec = pltpu.PrefetchScalarGridSpec(
        num_scalar_prefetch=2,
        grid=(Mp // te,),
        in_specs=[
            pl.BlockSpec((te, D // 2), lambda t, te_r, nt_r: (t, 0)),
            pl.BlockSpec((None, 1, D, D_EXPERT), lambda t, te_r, nt_r: (layer, te_r[t], 0, 0)),
            pl.BlockSpec((None, 1, D, D_EXPERT), lambda t, te_r, nt_r: (layer, te_r[t], 0, 0)),
            pl.BlockSpec((None, 1, D_EXPERT, D), lambda t, te_r, nt_r: (layer, te_r[t], 0, 0)),
        ],
        out_specs=pl.BlockSpec((te, D), lambda t, te_r, nt_r: (t, 0)),
        scratch_shapes=[pltpu.VMEM((D, D_EXPERT), BF16), pltpu.VMEM((D, D_EXPERT), BF16),
                        pltpu.VMEM((D_EXPERT, D), BF16)],
    )
    return pl.pallas_call(
        _expert_kernel,
        grid_spec=grid_spec,
        out_shape=jax.ShapeDtypeStruct((Mp, D), BF16),
        compiler_params=_cparams(("arbitrary",)),
        name="experts",
    )(tile_expert, n_tiles, xs, wg, wu, wd)


def _combine_kernel(x_ref, y0_ref, y1_ref, gate_ref, o_ref):
    gate = gate_ref[...]
    o_ref[...] = (x_ref[...] + gate[:, 0:1] * y0_ref[...].astype(F32)
                  + gate[:, 1:2] * y1_ref[...].astype(F32))


def _combine(x, y0, y1, gate):
    T, D = x.shape
    tm = ROW_TILE
    row = lambda wd: pl.BlockSpec((tm, wd), lambda i: (i, 0))
    return pl.pallas_call(
        _combine_kernel,
        grid=(T // tm,),
        in_specs=[row(D), row(D), row(D), row(ROUTER_COLS)],
        out_specs=row(D),
        out_shape=jax.ShapeDtypeStruct((T, D), F32),
        compiler_params=_cparams(("parallel",)),
        name="moe_combine",
    )(x, y0, y1, gate)


def _moe(x, p, layer):
    T, D = x.shape
    xn, gate, eid, counts_row = _router(x, p["ffn_norm"], p["r_whi"], p["r_wlo"], p["r_bias"])
    M = T * TOP_K
    te = EXPERT_TILE
    Mp = M + N_EXPERTS * te
    counts = counts_row[0, ROUTER_E0:ROUTER_E0 + N_EXPERTS].astype(jnp.int32)
    padded = ((counts + te - 1) // te) * te
    pend = jnp.cumsum(padded)
    pstart = pend - padded
    pstart_row = jnp.zeros((1, ROUTER_COLS), F32).at[0, ROUTER_E0:ROUTER_E0 + N_EXPERTS].set(pstart.astype(F32))
    slot = _slots(eid, pstart_row)[:, :TOP_K]
    tok_of_slot = jnp.zeros((Mp,), jnp.int32).at[slot.reshape(M)].set(
        jnp.arange(M, dtype=jnp.int32) // TOP_K, mode="promise_in_bounds", unique_indices=True)
    tile_ids = jnp.arange(Mp // te, dtype=jnp.int32)
    tile_expert = jnp.minimum(jnp.sum((pend // te)[None, :] <= tile_ids[:, None], axis=1),
                              N_EXPERTS - 1).astype(jnp.int32)
    n_tiles = (pend[-1] // te).astype(jnp.int32).reshape(1)
    rows = lambda a, idx: a.at[idx].get(mode="promise_in_bounds")
    xs = rows(xn, tok_of_slot)
    ys = _experts(tile_expert, n_tiles, xs, p["w_gate"], p["w_up"], p["w_down"], layer)
    return _combine(x, rows(ys, slot[:, 0]), rows(ys, slot[:, 1]), gate)


def _rope_tables(seq):
    half = HEAD_DIM // 2
    inv = ROPE_THETA ** (-jnp.arange(half, dtype=F32) / half)
    ang = jnp.arange(seq, dtype=F32)[:, None] * inv[None, :]
    cos = jnp.tile(jnp.cos(ang), (1, LANES // half))
    sin = jnp.sin(ang)
    sin = jnp.tile(jnp.concatenate([-sin, sin], axis=1), (1, LANES // HEAD_DIM))
    return cos, sin


def _head_gain(g, heads):
    return jnp.tile(g.astype(F32), (heads,))


def _prepare(ev_norm, ev_w_in, ev_qn_a, ev_kn_a, ev_qn_b, ev_kn_b, ev_sink, ev_w_out,
             od_norm, od_w_in, od_qn, od_kn, od_lq1, od_lk1, od_lq2, od_lk2, od_subln, od_w_out,
             ffn_norm, rg_w, rg_b, re_w, re_b, w_gate, w_up, w_down):
    hd = HEAD_DIM
    P = {}
    r = jnp.arange(MXU_DIM) // hd
    P["bd"] = (r[:, None] == r[None, :]).astype(BF16)
    w = ev_w_in[0]
    a3 = 3 * A_HEADS * hd
    qb_end = a3 + B_Q_HEADS * hd
    kb = [w[:, qb_end + h * hd:qb_end + (h + 1) * hd] for h in range(B_KV_HEADS)]
    vb0 = qb_end + B_KV_HEADS * hd
    vb = [w[:, vb0 + h * hd:vb0 + (h + 1) * hd] for h in range(B_KV_HEADS)]
    P["ev_w_in"] = jnp.concatenate([w[:, :qb_end]] + [c for h in kb for c in (h, h)]
                                   + [c for h in vb for c in (h, h)], axis=1).astype(BF16)
    ones = jnp.ones((A_HEADS * hd,), F32)
    P["ev_hg"] = jnp.concatenate([
        _head_gain(ev_qn_a[0], A_HEADS), _head_gain(ev_kn_a[0], A_HEADS), ones,
        _head_gain(ev_qn_b[0], B_Q_HEADS), _head_gain(ev_kn_b[0], 2 * B_KV_HEADS),
        jnp.ones((2 * B_KV_HEADS * hd,), F32)])[None, :]
    P["ev_norm"] = ev_norm[0][None, :]
    P["ev_sink"] = ev_sink[0].astype(F32)
    P["ev_w_out"] = ev_w_out[0].astype(BF16)
    P["od_w_in"] = od_w_in[0].astype(BF16)
    P["od_hg"] = jnp.concatenate([_head_gain(od_qn[0], 2 * C_HEADS), _head_gain(od_kn[0], 2 * C_HEADS),
                                  jnp.ones((2 * C_HEADS * hd,), F32)])[None, :]
    P["od_norm"] = od_norm[0][None, :]
    P["od_lam"] = jnp.stack([od_lq1[0], od_lk1[0], od_lq2[0], od_lk2[0]]).astype(F32)
    P["od_subln"] = od_subln[0][:, None].astype(F32)
    P["od_w_out"] = od_w_out[0].astype(BF16)
    P["moe"] = []
    for layer in range(ffn_norm.shape[0]):
        wr = jnp.zeros((D_MODEL, ROUTER_COLS), F32)
        wr = wr.at[:, :N_GROUPS].set(rg_w[layer])
        we = jnp.transpose(re_w[layer], (1, 0, 2)).reshape(D_MODEL, N_EXPERTS)
        wr = wr.at[:, ROUTER_E0:ROUTER_E0 + N_EXPERTS].set(we)
        bias = jnp.zeros((ROUTER_COLS,), F32).at[:N_GROUPS].set(rg_b[layer])
        bias = bias.at[ROUTER_E0:ROUTER_E0 + N_EXPERTS].set(re_b[layer].reshape(N_EXPERTS))
        whi = wr.astype(BF16)
        P["moe"].append({
            "ffn_norm": ffn_norm[layer][None, :],
            "r_whi": whi, "r_wlo": (wr - whi.astype(F32)).astype(BF16), "r_bias": bias[None, :],
            "w_gate": w_gate, "w_up": w_up, "w_down": w_down,
        })
    return P


def _even_mixer(x, P, B, S):
    T = B * S
    hd = HEAD_DIM
    aw = A_HEADS * hd
    cos_t, sin_t = _rope_tables(S)
    scale = hd ** -0.5
    tm = ROW_TILE
    dils = [d for _, d in DILATION_GROUPS]
    assert dils[0] == 1
    out_blocks = [(tm, aw)] * 4 + [(tm, 2 * LANES)] * 2
    views = []
    for _ in range(3):
        tv = []
        for d in dils[1:]:
            tv.append((d, len(out_blocks)))
            out_blocks.append((tm // d, d * aw))
        views.append(tuple(tv))
    plan = ((0, 0, aw, True, scale, views[0]), (1, aw, aw, True, 1.0, views[1]),
            (2, 2 * aw, aw, False, 1.0, views[2]), (3, 3 * aw, aw, True, scale, ()),
            (4, 4 * aw, 2 * LANES, True, 1.0, ()), (5, 4 * aw + 2 * LANES, 2 * LANES, False, 1.0, ()))
    res = _inproj(x, P["ev_norm"], P["ev_w_in"], P["ev_hg"], cos_t, sin_t, P["bd"], plan, out_blocks, S,
                  stage_width=aw)
    qb, kb, vb = res[3:6]
    outs, lses = [], []
    for gi, (window, dil) in enumerate(DILATION_GROUPS):
        L = S // dil
        qkv = [res[t] if dil == 1 else res[views[t][gi - 1][1]] for t in range(3)]
        o, lse = _band_a(*[t.reshape(B, L, dil * aw) for t in qkv], window // (2 * dil))
        outs.append(o.reshape(T // dil, dil * aw))
        lses.append(lse.reshape(T // dil, dil * aw))
    ob = _band_b(P["ev_sink"], qb.reshape(B, S, aw), kb.reshape(B, S, 2 * LANES), vb.reshape(B, S, 2 * LANES))
    return _outproj_even(outs[0], outs[1], outs[2], lses[0], lses[1], lses[2], ob.reshape(T, aw),
                         P["ev_w_out"], x)


def _diff_mixer(x, P, B, S, lam_init):
    T = B * S
    hd = HEAD_DIM
    cw = 2 * C_HEADS * hd
    cos_t, sin_t = _rope_tables(S)
    plan = ((0, 0, cw, True, LOG2_E * hd ** -0.5, ()), (1, cw, cw, True, 1.0, ()), (2, 2 * cw, cw, False, 1.0, ()))
    q, k, v = _inproj(x, P["od_norm"], P["od_w_in"], P["od_hg"], cos_t, sin_t, P["bd"], plan,
                      [(ROW_TILE, cw)] * 3, S)
    tk = DIFF_TK
    qT = q.reshape(B, S, C_HEADS, LANES).transpose(0, 2, 3, 1)
    vT = v.reshape(B, S // tk, tk, C_HEADS, LANES).transpose(0, 3, 1, 4, 2)
    o = _diff_attention(P["od_lam"], qT, k.reshape(B, S, cw), vT, P["od_subln"], lam_init)
    return _outproj(o.reshape(T, cw), P["od_w_out"], x)


def _trunk(x3, P):
    B, S, D = x3.shape
    x = x3.reshape(B * S, D)
    x = _even_mixer(x, P, B, S)
    x = _moe(x, P["moe"][0], 0)
    lam_init = 0.8 - 0.6 * math.exp(-0.3 * 1)
    x = _diff_mixer(x, P, B, S, lam_init)
    x = _moe(x, P["moe"][1], 1)
    return x.reshape(B, S, D)


def kernel(x_prompt, x_sample, ev_norm, ev_w_in, ev_qn_a, ev_kn_a, ev_qn_b, ev_kn_b, ev_sink, ev_w_out,
           od_norm, od_w_in, od_qn, od_kn, od_lq1, od_lk1, od_lq2, od_lk2, od_subln, od_w_out,
           ffn_norm, rg_w, rg_b, re_w, re_b, w_gate, w_up, w_down):
    P = _prepare(ev_norm, ev_w_in, ev_qn_a, ev_kn_a, ev_qn_b, ev_kn_b, ev_sink, ev_w_out,
                 od_norm, od_w_in, od_qn, od_kn, od_lq1, od_lk1, od_lq2, od_lk2, od_subln, od_w_out,
                 ffn_norm, rg_w, rg_b, re_w, re_b, w_gate, w_up, w_down)
    return (_trunk(x_prompt, P), _trunk(x_sample, P))
```

```python
import functools
import math

import jax
import jax.numpy as jnp
from jax import lax
from jax.experimental import pallas as pl
from jax.experimental.pallas import tpu as pltpu

F32 = jnp.float32
BF16 = jnp.bfloat16

D_MODEL = 1024
HEAD_DIM = 64
A_HEADS = 8
DILATION_GROUPS = ((128, 1), (512, 4), (2048, 16))
B_Q_HEADS = 8
B_KV_HEADS = 2
B_HALF_WINDOW = 128
C_HEADS = 8
N_GROUPS = 4
EXPERTS_PER_GROUP = 8
N_EXPERTS = N_GROUPS * EXPERTS_PER_GROUP
TOP_K = 2
D_EXPERT = 512
ROPE_THETA = 10000.0
EPS = 1e-6
NEG_INF = -1e30

LANES = 128
MXU_DIM = 256
VMEM_LIMIT_BYTES = 52 * 1024 * 1024

ROW_TILE = 512
COL_CHUNK = 512
BAND_A_TQ = 128
BAND_TQ = 256
BAND_KBLK = 128
BAND_ONES_ROWS = 16
BAND_UNROLL = 2
BAND_BLOCK_BYTES =2 * 1024 * 1024
DIFF_TQ = 512
DIFF_TK = 256
DIFF_UNROLL = 16
DIFF_ONES_ROWS = 16
LOG2_E = 1.4426950408889634
EXPERT_TILE = 256
ROUTER_COLS = 128
ROUTER_E0 = 8
DISPATCH_TABLE_ROWS = 32768


def _cparams(sem):
    return pltpu.CompilerParams(dimension_semantics=sem, vmem_limit_bytes=VMEM_LIMIT_BYTES)


def _lane_iota(shape):
    return lax.broadcasted_iota(jnp.int32, shape, len(shape) - 1)


def _inproj_kernel(x_ref, g_ref, w_ref, hg_ref, cos_ref, sin_ref, bd_ref, *refs, plan, n_out):
    out_refs = refs[:n_out]
    stage_ref = refs[n_out] if len(refs) > n_out else None
    x = x_ref[...]
    tm = x.shape[0]
    ms = jnp.mean(x * x, axis=-1, keepdims=True)
    xn = (x * lax.rsqrt(ms + EPS) * g_ref[...]).astype(BF16)
    lane = _lane_iota((1, LANES))
    first_half = (lane % HEAD_DIM) < (HEAD_DIM // 2)
    cos = cos_ref[...]
    sin = sin_ref[...]
    for oi, c0, width, normed, scale, views in plan:
        for cc in range(0, width, COL_CHUNK):
            cw = min(COL_CHUNK, width - cc)
            y = jnp.dot(xn, w_ref[:, c0 + cc:c0 + cc + cw], preferred_element_type=F32)
            if normed:
                parts = []
                for n0 in range(0, cw, MXU_DIM):
                    nw = min(MXU_DIM, cw - n0)
                    yn = y[:, n0:n0 + nw]
                    hs = jnp.dot((yn * yn).astype(BF16), bd_ref[:nw, :nw], preferred_element_type=F32)
                    parts.append(yn * lax.rsqrt(hs * (1.0 / HEAD_DIM) + EPS)
                                 * hg_ref[:, c0 + cc + n0:c0 + cc + n0 + nw])
                y = parts[0] if len(parts) == 1 else jnp.concatenate(parts, axis=1)
            for j in range(cw // LANES):
                r = y[:, j * LANES:(j + 1) * LANES]
                if normed:
                    rot = jnp.where(first_half, pltpu.roll(r, LANES - HEAD_DIM // 2, 1),
                                    pltpu.roll(r, HEAD_DIM // 2, 1))
                    r = r * cos + rot * sin
                    if scale != 1.0:
                        r = r * scale
                cols = slice(cc + j * LANES, cc + (j + 1) * LANES)
                out_refs[oi][:, cols] = r.astype(BF16)
                if views:
                    stage_ref[cc // LANES + j] = r
        for d, vi in views:
            for rr in range(d):
                for j in range(width // LANES):
                    out_refs[vi][:, rr * width + j * LANES:rr * width + (j + 1) * LANES] = (
                        stage_ref[j, pl.ds(rr, tm // d, stride=d), :].astype(BF16))


def _inproj(x, g, w, hg, cos_t, sin_t, bd, plan, out_blocks, seq, stage_width=0):
    T, D = x.shape
    N = w.shape[1]
    tm = ROW_TILE
    n_pos_blocks = seq // tm
    const = lambda i: (0, 0)
    scratch = [pltpu.VMEM((stage_width // LANES, tm, LANES), F32)] if stage_width else []
    return pl.pallas_call(
        functools.partial(_inproj_kernel, plan=plan, n_out=len(out_blocks)),
        grid=(T // tm,),
        in_specs=[
            pl.BlockSpec((tm, D), lambda i: (i, 0)),
            pl.BlockSpec((1, D), const),
            pl.BlockSpec((D, N), const),
            pl.BlockSpec((1, N), const),
            pl.BlockSpec((tm, LANES), lambda i: (i % n_pos_blocks, 0)),
            pl.BlockSpec((tm, LANES), lambda i: (i % n_pos_blocks, 0)),
            pl.BlockSpec((MXU_DIM, MXU_DIM), const),
        ],
        out_specs=[pl.BlockSpec((br, bc), lambda i: (i, 0)) for br, bc in out_blocks],
        out_shape=[jax.ShapeDtypeStruct((T // tm * br, bc), BF16) for br, bc in out_blocks],
        scratch_shapes=scratch,
        compiler_params=_cparams(("parallel",)),
        name="inproj",
    )(x, g, w, hg, cos_t, sin_t, bd)


def _band_window(i, tq, hw, L):
    win = tq + 2 * hw
    start = jnp.clip(i * tq - hw, 0, L - win)
    start = pl.multiple_of(start, 64)
    qpos = i * tq + lax.broadcasted_iota(jnp.int32, (tq, win), 0)
    kpos = start + lax.broadcasted_iota(jnp.int32, (tq, win), 1)
    valid = jnp.abs(kpos - qpos) <= hw
    return start, win, valid


def _band_a_kernel(q_ref, k_ref, v_ref, o_ref, lse_ref, *, hw, tq, L, wb):
    lane = _lane_iota((1, LANES))
    head0 = lane < HEAD_DIM

    def qblock(i, carry):
        start, win, valid = _band_window(i, tq, hw, L)
        rows = pl.ds(pl.multiple_of(i * tq, tq), tq)
        for sl in range(wb // LANES):
            cols = slice(sl * LANES, (sl + 1) * LANES)
            q = q_ref[0, rows, cols]
            kw = k_ref[0, pl.ds(start, win), cols]
            vw = v_ref[0, pl.ds(start, win), cols]
            outs = []
            lses = []
            for hmask in (head0, jnp.logical_not(head0)):
                qh = jnp.where(hmask, q, jnp.zeros_like(q))
                s = lax.dot_general(qh, kw, (((1,), (1,)), ((), ())), preferred_element_type=F32)
                s = jnp.where(valid, s, NEG_INF)
                m = jnp.max(s, axis=-1, keepdims=True)
                p = jnp.exp(s - m)
                den = jnp.sum(p, axis=-1, keepdims=True)
                pv = jnp.dot(p.astype(BF16), vw, preferred_element_type=F32)
                outs.append(pv / den)
                lses.append(m + jnp.log(den))
            o_ref[0, rows, cols] = jnp.where(head0, outs[0], outs[1]).astype(o_ref.dtype)
            lse_ref[0, rows, cols] = jnp.where(head0, lses[0], lses[1])
        return carry

    lax.fori_loop(0, L // tq, qblock, 0, unroll=BAND_UNROLL)


def _band_a(q, k, v, hw):
    NB, L, C = q.shape
    tq = BAND_A_TQ
    assert L >= tq + 2 * hw and L % tq == 0
    wb = min(C, A_HEADS * HEAD_DIM)
    while L * wb * 2 > BAND_BLOCK_BYTES and wb > LANES:
        wb //= 2
    spec = pl.BlockSpec((1, L, wb), lambda n, c: (n, 0, c))
    return pl.pallas_call(
        functools.partial(_band_a_kernel, hw=hw, tq=tq, L=L, wb=wb),
        grid=(NB, C // wb),
        in_specs=[spec, spec, spec],
        out_specs=[spec, spec],
        out_shape=[jax.ShapeDtypeStruct((NB, L, C), BF16), jax.ShapeDtypeStruct((NB, L, C), F32)],
        compiler_params=_cparams(("parallel", "parallel")),
        name="band_a",
    )(q, k, v)


def _band_core(i, L, hw, q_heads, k_ref, kcols, vT_scr, vslab, sink_row):
    tq = min(BAND_TQ, L)
    kb = BAND_KBLK
    nblk = L // kb
    wblk = min(tq // kb + 2, nblk)
    win = wblk * kb
    blk0 = jnp.clip(i * (tq // kb) - 1, 0, nblk - wblk)
    kw = k_ref[0, pl.ds(pl.multiple_of(blk0 * kb, kb), win), kcols]
    qq = jnp.concatenate(q_heads, axis=0)
    s = lax.dot_general(kw, qq, (((1,), (1,)), ((), ())), preferred_element_type=F32)
    kpos = blk0 * kb + lax.broadcasted_iota(jnp.int32, (win, tq), 0)
    qpos = i * tq + lax.broadcasted_iota(jnp.int32, (win, tq), 1)
    valid = jnp.abs(kpos - qpos) <= hw
    s = jnp.where(jnp.concatenate([valid] * len(q_heads), axis=1), s, NEG_INF)
    m = jnp.max(s, axis=0, keepdims=True)
    if sink_row is not None:
        m = jnp.maximum(m, sink_row)
    p = jnp.exp(s - m).astype(BF16)
    vT = jnp.concatenate([vT_scr[vslab, blk0 + w] for w in range(wblk)], axis=1)
    lhs = jnp.concatenate([vT, jnp.ones((BAND_ONES_ROWS, win), BF16)], axis=0)
    return jnp.dot(lhs, p, preferred_element_type=F32), m


def _transpose_v(v_ref, vT_scr, L, n_slabs):
    def blk_step(blk, carry):
        rows = pl.ds(pl.multiple_of(blk * BAND_KBLK, BAND_KBLK), BAND_KBLK)
        for sl in range(n_slabs):
            vb = v_ref[0, rows, sl * LANES:(sl + 1) * LANES].astype(F32)
            vT_scr[sl, blk] = vb.T.astype(BF16)
        return carry
    lax.fori_loop(0, L // BAND_KBLK, blk_step, 0)


def _band_b_kernel(sink_ref, q_ref, k_ref, v_ref, o_ref, vT_scr, *, hw, L):
    tq = min(BAND_TQ, L)
    lane = _lane_iota((1, LANES))
    head0 = lane < HEAD_DIM
    row0 = lax.broadcasted_iota(jnp.int32, (LANES, 1), 0) < HEAD_DIM
    hk = pl.program_id(1)
    group = B_Q_HEADS // B_KV_HEADS
    sink = jnp.concatenate(
        [jnp.full((1, tq), sink_ref[hk * group + g], F32) for g in range(group)], axis=1)
    _transpose_v(v_ref, vT_scr, L, 1)

    def qblock(i, carry):
        rows = pl.ds(pl.multiple_of(i * tq, tq), tq)
        qs = []
        for g in range(group):
            q = q_ref[0, rows, (g // 2) * LANES:(g // 2 + 1) * LANES]
            zero = jnp.zeros_like(q)
            qs.append(jnp.where(head0, q, zero) if g % 2 == 0 else jnp.where(head0, zero, q))
        acc, m = _band_core(i, L, hw, qs, k_ref, slice(0, LANES), vT_scr, 0, sink)
        den = acc[LANES:LANES + 1] + jnp.exp(sink - m)
        oT = acc[:LANES] / den
        for pair in range(group // 2):
            a = oT[:, (2 * pair) * tq:(2 * pair + 1) * tq]
            b = oT[:, (2 * pair + 1) * tq:(2 * pair + 2) * tq]
            o_ref[0, rows, pair * LANES:(pair + 1) * LANES] = jnp.where(row0, a, b).T.astype(o_ref.dtype)
        return carry

    lax.fori_loop(0, L // tq, qblock, 0, unroll=min(BAND_UNROLL, L // tq))


def _band_b(sink, q, kdup, vdup):
    B, S, C = q.shape
    tq = BAND_TQ
    hw = B_HALF_WINDOW
    qw = C // B_KV_HEADS
    assert hw <= BAND_KBLK and S % tq == 0
    return pl.pallas_call(
        functools.partial(_band_b_kernel, hw=hw, L=S),
        grid=(B, B_KV_HEADS),
        in_specs=[
            pl.BlockSpec(memory_space=pltpu.SMEM),
            pl.BlockSpec((1, S, qw), lambda b, h: (b, 0, h)),
            pl.BlockSpec((1, S, LANES), lambda b, h: (b, 0, h)),
            pl.BlockSpec((1, S, LANES), lambda b, h: (b, 0, h)),
        ],
        out_specs=pl.BlockSpec((1, S, qw), lambda b, h: (b, 0, h)),
        out_shape=jax.ShapeDtypeStruct((B, S, C), BF16),
        scratch_shapes=[pltpu.VMEM((1, S // BAND_KBLK, LANES, BAND_KBLK), BF16)],
        compiler_params=_cparams(("parallel", "parallel")),
        name="band_b",
    )(sink, q, kdup, vdup)


def _outproj_even_kernel(o1_ref, o4_ref, o16_ref, l1_ref, l4_ref, l16_ref, ob_ref, w_ref, x_ref, y_ref,
                         so4, sl4, so16, sl16):
    tm, half = o1_ref.shape

    def token_major(src_ref, stage_ref, d):
        for rr in range(d):
            for j in range(half // LANES):
                stage_ref[j, pl.ds(rr, tm // d, stride=d), :] = (
                    src_ref[:, rr * half + j * LANES:rr * half + (j + 1) * LANES].astype(F32))
        return jnp.concatenate([stage_ref[j] for j in range(half // LANES)], axis=1)

    dils = [d for _, d in DILATION_GROUPS]
    l1 = l1_ref[...]
    o4, l4 = token_major(o4_ref, so4, dils[1]), token_major(l4_ref, sl4, dils[1])
    o16, l16 = token_major(o16_ref, so16, dils[2]), token_major(l16_ref, sl16, dils[2])
    m = jnp.maximum(jnp.maximum(l1, l4), l16)
    e1, e4, e16 = jnp.exp(l1 - m), jnp.exp(l4 - m), jnp.exp(l16 - m)
    num = e1 * o1_ref[...].astype(F32) + e4 * o4 + e16 * o16
    oa = (num / (e1 + e4 + e16)).astype(BF16)
    ka = oa.shape[1]
    acc = jnp.dot(oa, w_ref[:ka, :], preferred_element_type=F32)
    acc = acc + jnp.dot(ob_ref[...], w_ref[ka:, :], preferred_element_type=F32)
    y_ref[...] = x_ref[...] + acc


def _outproj_even(o1, o4, o16, l1, l4, l16, ob, w, x):
    T, D = x.shape
    tm = ROW_TILE
    half = o1.shape[1]
    row = lambda wd: pl.BlockSpec((tm, wd), lambda i: (i, 0))
    view = lambda d: pl.BlockSpec((tm // d, d * half), lambda i: (i, 0))
    d4, d16 = DILATION_GROUPS[1][1], DILATION_GROUPS[2][1]
    return pl.pallas_call(
        _outproj_even_kernel,
        grid=(T // tm,),
        in_specs=[row(half), view(d4), view(d16), row(half), view(d4), view(d16), row(half),
                  pl.BlockSpec(w.shape, lambda i: (0, 0)), row(D)],
        out_specs=row(D),
        out_shape=jax.ShapeDtypeStruct((T, D), F32),
        scratch_shapes=[pltpu.VMEM((half // LANES, tm, LANES), F32)] * 4,
        compiler_params=_cparams(("parallel",)),
        name="outproj_even",
    )(o1, o4, o16, l1, l4, l16, ob, w, x)


def _outproj_kernel(o_ref, w_ref, x_ref, y_ref):
    y_ref[...] = x_ref[...] + jnp.dot(o_ref[...], w_ref[...], preferred_element_type=F32)


def _outproj(o, w, x):
    T, D = x.shape
    tm = ROW_TILE
    row = lambda wd: pl.BlockSpec((tm, wd), lambda i: (i, 0))
    return pl.pallas_call(
        _outproj_kernel,
        grid=(T // tm,),
        in_specs=[row(o.shape[1]), pl.BlockSpec(w.shape, lambda i: (0, 0)), row(D)],
        out_specs=row(D),
        out_shape=jax.ShapeDtypeStruct((T, D), F32),
        compiler_params=_cparams(("parallel",)),
        name="outproj",
    )(o, w, x)


def _diff_kernel(lam_ref, qT_ref, k_ref, vT_ref, sub_ref, o_ref, s0_scr, s1_scr, acc_scr, *,
                 tq, tk, S, lam_init, unroll):
    row = lax.broadcasted_iota(jnp.int32, (LANES, 1), 0)
    comp0 = row < HEAD_DIM
    lamv = lam_ref[...]
    lam = (jnp.exp(jnp.sum(lamv[0:1] * lamv[1:2], axis=-1, keepdims=True))
           - jnp.exp(jnp.sum(lamv[2:3] * lamv[3:4], axis=-1, keepdims=True)) + lam_init)
    qT = qT_ref[0, 0]
    zero = jnp.zeros_like(qT)
    qqT = jnp.concatenate([jnp.where(comp0, qT, zero), jnp.where(comp0, zero, qT)], axis=1)
    ones = jnp.ones((DIFF_ONES_ROWS, tk), BF16)
    n_kv = S // tk

    def scores(t, dst):
        kb = k_ref[0, pl.ds(pl.multiple_of(t * tk, tk), tk), :]
        dst[...] = jnp.dot(kb, qqT, preferred_element_type=F32)

    def softmax_pv(src, t, m):
        s = src[...]
        lhs = jnp.concatenate([vT_ref[0, 0, t], ones], axis=0)
        m_new = jnp.maximum(m, jnp.max(s, axis=0, keepdims=True))
        alpha = jnp.exp2(m - m_new)
        p = jnp.exp2(s - m_new).astype(BF16)
        acc_scr[...] = alpha * acc_scr[...] + jnp.dot(lhs, p, preferred_element_type=F32)
        return m_new

    def step(i, m):
        bufs = (s0_scr, s1_scr)
        for u in range(unroll):
            t = unroll * i + u
            scores(jnp.minimum(t + 1, n_kv - 1), bufs[(u + 1) % 2])
            m = softmax_pv(bufs[u % 2], t, m)
        return m

    acc_scr[...] = jnp.zeros_like(acc_scr)
    scores(0, s0_scr)
    lax.fori_loop(0, n_kv // unroll, step, jnp.full((1, 2 * tq), NEG_INF, F32))
    acc = acc_scr[...]
    a = acc[:LANES] / acc[LANES:LANES + 1]
    o = a[:, :tq] - lam * a[:, tq:]
    ms = jnp.mean(o * o, axis=0, keepdims=True)
    o = o * lax.rsqrt(ms + EPS) * sub_ref[...] * (1.0 - lam_init)
    o_ref[0] = o.T.astype(o_ref.dtype)


def _diff_attention(lam_vecs, qT, k, vT, subln_col, lam_init):
    B, H, _, S = qT.shape
    tq, tk = DIFF_TQ, DIFF_TK
    unroll = min(DIFF_UNROLL, S // tk)
    assert S % tq == 0 and S % (tk * unroll) == 0 and unroll % 2 == 0
    acc_rows = LANES + DIFF_ONES_ROWS
    return pl.pallas_call(
        functools.partial(_diff_kernel, tq=tq, tk=tk, S=S, lam_init=lam_init, unroll=unroll),
        grid=(B, H, S // tq),
        in_specs=[
            pl.BlockSpec(lam_vecs.shape, lambda b, h, i: (0, 0)),
            pl.BlockSpec((1, 1, LANES, tq), lambda b, h, i: (b, h, 0, i)),
            pl.BlockSpec((1, S, LANES), lambda b, h, i: (b, 0, h)),
            pl.BlockSpec((1, 1, S // tk, LANES, tk), lambda b, h, i: (b, h, 0, 0, 0)),
            pl.BlockSpec((LANES, 1), lambda b, h, i: (0, 0)),
        ],
        out_specs=pl.BlockSpec((1, tq, LANES), lambda b, h, i: (b, i, h)),
        out_shape=jax.ShapeDtypeStruct((B, S, H * LANES), BF16),
        scratch_shapes=[pltpu.VMEM((tk, 2 * tq), F32), pltpu.VMEM((tk, 2 * tq), F32),
                        pltpu.VMEM((acc_rows, 2 * tq), F32)],
        compiler_params=_cparams(("parallel", "parallel", "arbitrary")),
        name="diff_attn",
    )(lam_vecs, qT, k, vT, subln_col)


def _router_kernel(x_ref, g_ref, whi_ref, wlo_ref, b_ref, xn_ref, gate_ref, eid_ref, cnt_ref, *, n_token_tiles):
    i = pl.program_id(0)

    @pl.when(i < n_token_tiles)
    def _():
        _router_tile(x_ref, g_ref, whi_ref, wlo_ref, b_ref, xn_ref, gate_ref, eid_ref, cnt_ref)

    @pl.when(i >= n_token_tiles)
    def _():
        xn_ref[...] = jnp.zeros_like(xn_ref)


def _router_tile(x_ref, g_ref, whi_ref, wlo_ref, b_ref, xn_ref, gate_ref, eid_ref, cnt_ref):
    x = x_ref[...]
    ms = jnp.mean(x * x, axis=-1, keepdims=True)
    xn = x * lax.rsqrt(ms + EPS) * g_ref[...]
    xhi = xn.astype(BF16)
    half = xn.shape[1] // 2
    hi_bits = lax.bitcast_convert_type(xhi[:, :half].astype(F32), jnp.uint32)
    lo_bits = lax.bitcast_convert_type(xhi[:, half:].astype(F32), jnp.uint32)
    xn_ref[...] = hi_bits | (lo_bits >> 16)
    xlo = (xn - xhi.astype(F32)).astype(BF16)
    whi = whi_ref[...]
    logits = (jnp.dot(xhi, whi, preferred_element_type=F32)
              + jnp.dot(xlo, whi, preferred_element_type=F32)
              + jnp.dot(xhi, wlo_ref[...], preferred_element_type=F32)) + b_ref[...]
    lane = _lane_iota(logits.shape).astype(F32)
    big = float(4 * LANES)
    lg = jnp.where(lane < N_GROUPS, logits, NEG_INF)
    mg = jnp.max(lg, axis=-1, keepdims=True)
    g_sel = jnp.min(jnp.where(lg == mg, lane, big), axis=-1, keepdims=True)
    p_sel = 1.0 / jnp.sum(jnp.exp(lg - mg), axis=-1, keepdims=True)
    e_lo = ROUTER_E0 + g_sel * EXPERTS_PER_GROUP
    le = jnp.where((lane >= e_lo) & (lane < e_lo + EXPERTS_PER_GROUP), logits, NEG_INF)
    v1 = jnp.max(le, axis=-1, keepdims=True)
    i1 = jnp.min(jnp.where(le == v1, lane, big), axis=-1, keepdims=True)
    le2 = jnp.where(lane == i1, NEG_INF, le)
    v2 = jnp.max(le2, axis=-1, keepdims=True)
    i2 = jnp.min(jnp.where(le2 == v2, lane, big), axis=-1, keepdims=True)
    e2 = jnp.exp(v2 - v1)
    den = 1.0 + e2
    g1 = (1.0 / den) * p_sel
    g2 = (e2 / den) * p_sel
    gate_ref[...] = jnp.where(lane == 0, g1, jnp.where(lane == 1, g2, 0.0))
    eid = jnp.where(lane == 0, i1 - ROUTER_E0, jnp.where(lane == 1, i2 - ROUTER_E0, 0.0))
    eid_ref[...] = eid.astype(jnp.int32)
    chosen = ((lane == i1) | (lane == i2)).astype(F32)

    @pl.when(pl.program_id(0) == 0)
    def _():
        cnt_ref[...] = jnp.zeros_like(cnt_ref)

    cnt_ref[...] += jnp.sum(chosen, axis=0, keepdims=True)


def _router(x, g, whi, wlo, bias):
    T, D = x.shape
    tm = ROW_TILE
    n_tok = T // tm
    table_rows = max(T, DISPATCH_TABLE_ROWS)
    const = lambda i: (0, 0)
    row = lambda wd: pl.BlockSpec((tm, wd), lambda i: (jnp.minimum(i, n_tok - 1), 0))
    return pl.pallas_call(
        functools.partial(_router_kernel, n_token_tiles=n_tok),
        grid=(table_rows // tm,),
        in_specs=[row(D), pl.BlockSpec((1, D), const), pl.BlockSpec((D, ROUTER_COLS), const),
                  pl.BlockSpec((D, ROUTER_COLS), const), pl.BlockSpec((1, ROUTER_COLS), const)],
        out_specs=[pl.BlockSpec((tm, D // 2), lambda i: (i, 0)), row(ROUTER_COLS), row(ROUTER_COLS),
                   pl.BlockSpec((1, ROUTER_COLS), const)],
        out_shape=[jax.ShapeDtypeStruct((table_rows, D // 2), jnp.uint32),
                   jax.ShapeDtypeStruct((T, ROUTER_COLS), F32),
                   jax.ShapeDtypeStruct((T, ROUTER_COLS), jnp.int32),
                   jax.ShapeDtypeStruct((1, ROUTER_COLS), F32)],
        compiler_params=_cparams(("arbitrary",)),
        name="router",
    )(x, g, whi, wlo, bias)


def _slot_kernel(eid_ref, pstart_ref, slot_ref, run_ref):
    tm = eid_ref.shape[0]

    @pl.when(pl.program_id(0) == 0)
    def _():
        run_ref[...] = jnp.zeros_like(run_ref)

    eid = eid_ref[...]
    lane = _lane_iota(eid.shape)
    oh0 = lane == (eid[:, 0:1] + ROUTER_E0)
    oh1 = lane == (eid[:, 1:2] + ROUTER_E0)
    both = (oh0 | oh1).astype(BF16)
    r = lax.broadcasted_iota(jnp.int32, (tm, tm), 0)
    c = lax.broadcasted_iota(jnp.int32, (tm, tm), 1)
    earlier = (c < r).astype(BF16)
    before = jnp.dot(earlier, both, preferred_element_type=F32)
    base = pstart_ref[...] + run_ref[...] + before
    s0 = jnp.sum(jnp.where(oh0, base, 0.0), axis=-1, keepdims=True)
    s1 = jnp.sum(jnp.where(oh1, base, 0.0), axis=-1, keepdims=True)
    slot_ref[...] = jnp.where(lane == 0, s0, jnp.where(lane == 1, s1, 0.0)).astype(jnp.int32)
    run_ref[...] += jnp.sum(both.astype(F32), axis=0, keepdims=True)


def _slots(eid, pstart_row):
    T = eid.shape[0]
    tm = ROW_TILE
    return pl.pallas_call(
        _slot_kernel,
        grid=(T // tm,),
        in_specs=[pl.BlockSpec((tm, ROUTER_COLS), lambda i: (i, 0)),
                  pl.BlockSpec((1, ROUTER_COLS), lambda i: (0, 0))],
        out_specs=pl.BlockSpec((tm, ROUTER_COLS), lambda i: (i, 0)),
        out_shape=jax.ShapeDtypeStruct((T, ROUTER_COLS), jnp.int32),
        scratch_shapes=[pltpu.VMEM((1, ROUTER_COLS), F32)],
        compiler_params=_cparams(("arbitrary",)),
        name="slots",
    )(eid, pstart_row)


def _expert_kernel(te_ref, nt_ref, nxt_ref, par_ref, xs_ref, wg_hbm, wu_hbm, wd_hbm, y_ref,
                   wg_f, wu_f, wd_f, wg_s, wu_s, wd_s, sem, *, layer):
    t = pl.program_id(0)
    live = t < nt_ref[0]
    e = te_ref[t]
    new_expert = (t == 0) | (e != te_ref[jnp.maximum(t - 1, 0)])
    slot = par_ref[t]

    def weight_copies(expert, s):
        return (pltpu.make_async_copy(wg_hbm.at[layer, expert], wg_f.at[s], sem.at[s, 0]),
                pltpu.make_async_copy(wu_hbm.at[layer, expert], wu_f.at[s], sem.at[s, 1]),
                pltpu.make_async_copy(wd_hbm.at[layer, expert], wd_f.at[s], sem.at[s, 2]))

    @pl.when(live & (t == 0))
    def _():
        for c in weight_copies(e, slot):
            c.start()

    @pl.when(live & new_expert)
    def _():
        for c in weight_copies(e, slot):
            c.wait()
        wg_s[...] = wg_f[slot].astype(BF16)
        wu_s[...] = wu_f[slot].astype(BF16)
        wd_s[...] = wd_f[slot].astype(BF16)
        nxt = nxt_ref[t]

        @pl.when(nxt >= 0)
        def _():
            for c in weight_copies(nxt, 1 - slot):
                c.start()

    @pl.when(live)
    def _():
        words = xs_ref[...]
        hi = lax.bitcast_convert_type(words & jnp.uint32(0xFFFF0000), F32).astype(BF16)
        lo = lax.bitcast_convert_type(words << 16, F32).astype(BF16)
        xs = jnp.concatenate([hi, lo], axis=1)
        a = jnp.dot(xs, wg_s[...], preferred_element_type=F32)
        u = jnp.dot(xs, wu_s[...], preferred_element_type=F32)
        h = (a * jax.nn.sigmoid(a) * u).astype(BF16)
        y_ref[...] = jnp.dot(h, wd_s[...], preferred_element_type=F32).astype(y_ref.dtype)

    @pl.when(jnp.logical_not(live))
    def _():
        y_ref[...] = jnp.zeros_like(y_ref)


def _experts(tile_expert, n_tiles, next_expert, parity, xs, wg, wu, wd, layer):
    Mp = xs.shape[0]
    D = wg.shape[2]
    te = EXPERT_TILE
    grid_spec = pltpu.PrefetchScalarGridSpec(
        num_scalar_prefetch=4,
        grid=(Mp // te,),
        in_specs=[
            pl.BlockSpec((te, D // 2), lambda t, *_: (t, 0)),
            pl.BlockSpec(memory_space=pl.ANY),
            pl.BlockSpec(memory_space=pl.ANY),
            pl.BlockSpec(memory_space=pl.ANY),
        ],
        out_specs=pl.BlockSpec((te, D), lambda t, *_: (t, 0)),
        scratch_shapes=[pltpu.VMEM((2, D, D_EXPERT), F32), pltpu.VMEM((2, D, D_EXPERT), F32),
                        pltpu.VMEM((2, D_EXPERT, D), F32),
                        pltpu.VMEM((D, D_EXPERT), BF16), pltpu.VMEM((D, D_EXPERT), BF16),
                        pltpu.VMEM((D_EXPERT, D), BF16),
                        pltpu.SemaphoreType.DMA((2, 3))],
    )
    return pl.pallas_call(
        functools.partial(_expert_kernel, layer=layer),
        grid_spec=grid_spec,
        out_shape=jax.ShapeDtypeStruct((Mp, D), BF16),
        compiler_params=_cparams(("arbitrary",)),
        name="experts",
    )(tile_expert, n_tiles, next_expert, parity, xs, wg, wu, wd)


def _combine_kernel(x_ref, y0_ref, y1_ref, gate_ref, o_ref):
    gate = gate_ref[...]
    o_ref[...] = (x_ref[...] + gate[:, 0:1] * y0_ref[...].astype(F32)
                  + gate[:, 1:2] * y1_ref[...].astype(F32))


def _combine(x, y0, y1, gate):
    T, D = x.shape
    tm = ROW_TILE
    row = lambda wd: pl.BlockSpec((tm, wd), lambda i: (i, 0))
    return pl.pallas_call(
        _combine_kernel,
        grid=(T // tm,),
        in_specs=[row(D), row(D), row(D), row(ROUTER_COLS)],
        out_specs=row(D),
        out_shape=jax.ShapeDtypeStruct((T, D), F32),
        compiler_params=_cparams(("parallel",)),
        name="moe_combine",
    )(x, y0, y1, gate)


def _moe(x, p, layer):
    T, D = x.shape
    xn, gate, eid, counts_row = _router(x, p["ffn_norm"], p["r_whi"], p["r_wlo"], p["r_bias"])
    M = T * TOP_K
    te = EXPERT_TILE
    Mp = M + N_EXPERTS * te
    counts = counts_row[0, ROUTER_E0:ROUTER_E0 + N_EXPERTS].astype(jnp.int32)
    padded = ((counts + te - 1) // te) * te
    pend = jnp.cumsum(padded)
    pstart = pend - padded
    pstart_row = jnp.zeros((1, ROUTER_COLS), F32).at[0, ROUTER_E0:ROUTER_E0 + N_EXPERTS].set(pstart.astype(F32))
    slot = _slots(eid, pstart_row)[:, :TOP_K]
    tok_of_slot = jnp.zeros((Mp,), jnp.int32).at[slot.reshape(M)].set(
        jnp.arange(M, dtype=jnp.int32) // TOP_K, mode="promise_in_bounds", unique_indices=True)
    tile_ids = jnp.arange(Mp // te, dtype=jnp.int32)
    tile_expert = jnp.minimum(jnp.sum((pend // te)[None, :] <= tile_ids[:, None], axis=1),
                              N_EXPERTS - 1).astype(jnp.int32)
    n_tiles = (pend[-1] // te).astype(jnp.int32).reshape(1)
    present = counts > 0
    ids = jnp.arange(N_EXPERTS, dtype=jnp.int32)
    later = (ids[None, :] > ids[:, None]) & present[None, :]
    next_present = jnp.min(jnp.where(later, ids[None, :], N_EXPERTS), axis=1)
    next_present = jnp.where(next_present == N_EXPERTS, -1, next_present).astype(jnp.int32)
    rank = (jnp.cumsum(present.astype(jnp.int32)) - present.astype(jnp.int32))
    next_expert = next_present[tile_expert]
    parity = (rank[tile_expert] % 2).astype(jnp.int32)
    rows = lambda a, idx: a.at[idx].get(mode="promise_in_bounds")
    xs = rows(xn, tok_of_slot)
    ys = _experts(tile_expert, n_tiles, next_expert, parity, xs, p["w_gate"], p["w_up"], p["w_down"], layer)
    return _combine(x, rows(ys, slot[:, 0]), rows(ys, slot[:, 1]), gate)


def _rope_tables(seq):
    half = HEAD_DIM // 2
    inv = ROPE_THETA ** (-jnp.arange(half, dtype=F32) / half)
    ang = jnp.arange(seq, dtype=F32)[:, None] * inv[None, :]
    cos = jnp.tile(jnp.cos(ang), (1, LANES // half))
    sin = jnp.sin(ang)
    sin = jnp.tile(jnp.concatenate([-sin, sin], axis=1), (1, LANES // HEAD_DIM))
    return cos, sin


def _head_gain(g, heads):
    return jnp.tile(g.astype(F32), (heads,))


def _prepare(ev_norm, ev_w_in, ev_qn_a, ev_kn_a, ev_qn_b, ev_kn_b, ev_sink, ev_w_out,
             od_norm, od_w_in, od_qn, od_kn, od_lq1, od_lk1, od_lq2, od_lk2, od_subln, od_w_out,
             ffn_norm, rg_w, rg_b, re_w, re_b, w_gate, w_up, w_down):
    hd = HEAD_DIM
    P = {}
    r = jnp.arange(MXU_DIM) // hd
    P["bd"] = (r[:, None] == r[None, :]).astype(BF16)
    w = ev_w_in[0]
    a3 = 3 * A_HEADS * hd
    qb_end = a3 + B_Q_HEADS * hd
    kb = [w[:, qb_end + h * hd:qb_end + (h + 1) * hd] for h in range(B_KV_HEADS)]
    vb0 = qb_end + B_KV_HEADS * hd
    vb = [w[:, vb0 + h * hd:vb0 + (h + 1) * hd] for h in range(B_KV_HEADS)]
    P["ev_w_in"] = jnp.concatenate([w[:, :qb_end]] + [c for h in kb for c in (h, h)]
                                   + [c for h in vb for c in (h, h)], axis=1).astype(BF16)
    ones = jnp.ones((A_HEADS * hd,), F32)
    P["ev_hg"] = jnp.concatenate([
        _head_gain(ev_qn_a[0], A_HEADS), _head_gain(ev_kn_a[0], A_HEADS), ones,
        _head_gain(ev_qn_b[0], B_Q_HEADS), _head_gain(ev_kn_b[0], 2 * B_KV_HEADS),
        jnp.ones((2 * B_KV_HEADS * hd,), F32)])[None, :]
    P["ev_norm"] = ev_norm[0][None, :]
    P["ev_sink"] = ev_sink[0].astype(F32)
    P["ev_w_out"] = ev_w_out[0].astype(BF16)
    P["od_w_in"] = od_w_in[0].astype(BF16)
    P["od_hg"] = jnp.concatenate([_head_gain(od_qn[0], 2 * C_HEADS), _head_gain(od_kn[0], 2 * C_HEADS),
                                  jnp.ones((2 * C_HEADS * hd,), F32)])[None, :]
    P["od_norm"] = od_norm[0][None, :]
    P["od_lam"] = jnp.stack([od_lq1[0], od_lk1[0], od_lq2[0], od_lk2[0]]).astype(F32)
    P["od_subln"] = od_subln[0][:, None].astype(F32)
    P["od_w_out"] = od_w_out[0].astype(BF16)
    P["moe"] = []
    for layer in range(ffn_norm.shape[0]):
        wr = jnp.zeros((D_MODEL, ROUTER_COLS), F32)
        wr = wr.at[:, :N_GROUPS].set(rg_w[layer])
        we = jnp.transpose(re_w[layer], (1, 0, 2)).reshape(D_MODEL, N_EXPERTS)
        wr = wr.at[:, ROUTER_E0:ROUTER_E0 + N_EXPERTS].set(we)
        bias = jnp.zeros((ROUTER_COLS,), F32).at[:N_GROUPS].set(rg_b[layer])
        bias = bias.at[ROUTER_E0:ROUTER_E0 + N_EXPERTS].set(re_b[layer].reshape(N_EXPERTS))
        whi = wr.astype(BF16)
        P["moe"].append({
            "ffn_norm": ffn_norm[layer][None, :],
            "r_whi": whi, "r_wlo": (wr - whi.astype(F32)).astype(BF16), "r_bias": bias[None, :],
            "w_gate": w_gate, "w_up": w_up, "w_down": w_down,
        })
    return P


def _even_mixer(x, P, B, S):
    T = B * S
    hd = HEAD_DIM
    aw = A_HEADS * hd
    cos_t, sin_t = _rope_tables(S)
    scale = hd ** -0.5
    tm = ROW_TILE
    dils = [d for _, d in DILATION_GROUPS]
    assert dils[0] == 1
    out_blocks = [(tm, aw)] * 4 + [(tm, 2 * LANES)] * 2
    views = []
    for _ in range(3):
        tv = []
        for d in dils[1:]:
            tv.append((d, len(out_blocks)))
            out_blocks.append((tm // d, d * aw))
        views.append(tuple(tv))
    plan = ((0, 0, aw, True, scale, views[0]), (1, aw, aw, True, 1.0, views[1]),
            (2, 2 * aw, aw, False, 1.0, views[2]), (3, 3 * aw, aw, True, scale, ()),
            (4, 4 * aw, 2 * LANES, True, 1.0, ()), (5, 4 * aw + 2 * LANES, 2 * LANES, False, 1.0, ()))
    res = _inproj(x, P["ev_norm"], P["ev_w_in"], P["ev_hg"], cos_t, sin_t, P["bd"], plan, out_blocks, S,
                  stage_width=aw)
    qb, kb, vb = res[3:6]
    outs, lses = [], []
    for gi, (window, dil) in enumerate(DILATION_GROUPS):
        L = S // dil
        qkv = [res[t] if dil == 1 else res[views[t][gi - 1][1]] for t in range(3)]
        o, lse = _band_a(*[t.reshape(B, L, dil * aw) for t in qkv], window // (2 * dil))
        outs.append(o.reshape(T // dil, dil * aw))
        lses.append(lse.reshape(T // dil, dil * aw))
    ob = _band_b(P["ev_sink"], qb.reshape(B, S, aw), kb.reshape(B, S, 2 * LANES), vb.reshape(B, S, 2 * LANES))
    return _outproj_even(outs[0], outs[1], outs[2], lses[0], lses[1], lses[2], ob.reshape(T, aw),
                         P["ev_w_out"], x)


def _diff_mixer(x, P, B, S, lam_init):
    T = B * S
    hd = HEAD_DIM
    cw = 2 * C_HEADS * hd
    cos_t, sin_t = _rope_tables(S)
    plan = ((0, 0, cw, True, LOG2_E * hd ** -0.5, ()), (1, cw, cw, True, 1.0, ()), (2, 2 * cw, cw, False, 1.0, ()))
    q, k, v = _inproj(x, P["od_norm"], P["od_w_in"], P["od_hg"], cos_t, sin_t, P["bd"], plan,
                      [(ROW_TILE, cw)] * 3, S)
    tk = DIFF_TK
    qT = q.reshape(B, S, C_HEADS, LANES).transpose(0, 2, 3, 1)
    vT = v.reshape(B, S // tk, tk, C_HEADS, LANES).transpose(0, 3, 1, 4, 2)
    o = _diff_attention(P["od_lam"], qT, k.reshape(B, S, cw), vT, P["od_subln"], lam_init)
    return _outproj(o.reshape(T, cw), P["od_w_out"], x)


def _trunk(x3, P):
    B, S, D = x3.shape
    x = x3.reshape(B * S, D)
    x = _even_mixer(x, P, B, S)
    x = _moe(x, P["moe"][0], 0)
    lam_init = 0.8 - 0.6 * math.exp(-0.3 * 1)
    x = _diff_mixer(x, P, B, S, lam_init)
    x = _moe(x, P["moe"][1], 1)
    return x.reshape(B, S, D)


def kernel(x_prompt, x_sample, ev_norm, ev_w_in, ev_qn_a, ev_kn_a, ev_qn_b, ev_kn_b, ev_sink, ev_w_out,
           od_norm, od_w_in, od_qn, od_kn, od_lq1, od_lk1, od_lq2, od_lk2, od_subln, od_w_out,
           ffn_norm, rg_w, rg_b, re_w, re_b, w_gate, w_up, w_down):
    P = _prepare(ev_norm, ev_w_in, ev_qn_a, ev_kn_a, ev_qn_b, ev_kn_b, ev_sink, ev_w_out,
                 od_norm, od_w_in, od_qn, od_kn, od_lq1, od_lk1, od_lq2, od_lk2, od_subln, od_w_out,
                 ffn_norm, rg_w, rg_b, re_w, re_b, w_gate, w_up, w_down)
    return (_trunk(x_prompt, P), _trunk(x_sample, P))
```

```python
import functools
import math

import jax
import jax.numpy as jnp
from jax import lax
from jax.experimental import pallas as pl
from jax.experimental.pallas import tpu as pltpu

F32 = jnp.float32
BF16 = jnp.bfloat16

D_MODEL = 1024
HEAD_DIM = 64
A_HEADS = 8
DILATION_GROUPS = ((128, 1), (512, 4), (2048, 16))
B_Q_HEADS = 8
B_KV_HEADS = 2
B_HALF_WINDOW = 128
C_HEADS = 8
N_GROUPS = 4
EXPERTS_PER_GROUP = 8
N_EXPERTS = N_GROUPS * EXPERTS_PER_GROUP
TOP_K = 2
D_EXPERT = 512
ROPE_THETA = 10000.0
EPS = 1e-6
NEG_INF = -1e30

LANES = 128
MXU_DIM = 256
VMEM_LIMIT_BYTES = 52 * 1024 * 1024

ROW_TILE = 512
COL_CHUNK = 512
BAND_A_TQ = 128
BAND_TQ = 256
BAND_KBLK = 128
BAND_ONES_ROWS = 16
BAND_UNROLL = 2
BAND_BLOCK_BYTES =2 * 1024 * 1024
DIFF_TQ = 512
DIFF_TK = 256
DIFF_UNROLL = 16
DIFF_ONES_ROWS = 16
LOG2_E = 1.4426950408889634
EXPERT_TILE = 256
ROUTER_COLS = 128
ROUTER_E0 = 8
DISPATCH_TABLE_ROWS = 32768


def _cparams(sem):
    return pltpu.CompilerParams(dimension_semantics=sem, vmem_limit_bytes=VMEM_LIMIT_BYTES)


def _lane_iota(shape):
    return lax.broadcasted_iota(jnp.int32, shape, len(shape) - 1)


def _inproj_kernel(x_ref, g_ref, w_ref, hg_ref, cos_ref, sin_ref, bd_ref, *refs, plan, n_out):
    out_refs = refs[:n_out]
    stage_ref = refs[n_out] if len(refs) > n_out else None
    x = x_ref[...]
    tm = x.shape[0]
    ms = jnp.mean(x * x, axis=-1, keepdims=True)
    xn = (x * lax.rsqrt(ms + EPS) * g_ref[...]).astype(BF16)
    lane = _lane_iota((1, LANES))
    first_half = (lane % HEAD_DIM) < (HEAD_DIM // 2)
    cos = cos_ref[...]
    sin = sin_ref[...]
    for oi, c0, width, normed, scale, views in plan:
        for cc in range(0, width, COL_CHUNK):
            cw = min(COL_CHUNK, width - cc)
            y = jnp.dot(xn, w_ref[:, c0 + cc:c0 + cc + cw], preferred_element_type=F32)
            if normed:
                parts = []
                for n0 in range(0, cw, MXU_DIM):
                    nw = min(MXU_DIM, cw - n0)
                    yn = y[:, n0:n0 + nw]
                    hs = jnp.dot((yn * yn).astype(BF16), bd_ref[:nw, :nw], preferred_element_type=F32)
                    parts.append(yn * lax.rsqrt(hs * (1.0 / HEAD_DIM) + EPS)
                                 * hg_ref[:, c0 + cc + n0:c0 + cc + n0 + nw])
                y = parts[0] if len(parts) == 1 else jnp.concatenate(parts, axis=1)
            for j in range(cw // LANES):
                r = y[:, j * LANES:(j + 1) * LANES]
                if normed:
                    rot = jnp.where(first_half, pltpu.roll(r, LANES - HEAD_DIM // 2, 1),
                                    pltpu.roll(r, HEAD_DIM // 2, 1))
                    r = r * cos + rot * sin
                    if scale != 1.0:
                        r = r * scale
                cols = slice(cc + j * LANES, cc + (j + 1) * LANES)
                out_refs[oi][:, cols] = r.astype(BF16)
                if views:
                    stage_ref[cc // LANES + j] = r
        for d, vi in views:
            for rr in range(d):
                for j in range(width // LANES):
                    out_refs[vi][:, rr * width + j * LANES:rr * width + (j + 1) * LANES] = (
                        stage_ref[j, pl.ds(rr, tm // d, stride=d), :].astype(BF16))


def _inproj(x, g, w, hg, cos_t, sin_t, bd, plan, out_blocks, seq, stage_width=0):
    T, D = x.shape
    N = w.shape[1]
    tm = ROW_TILE
    n_pos_blocks = seq // tm
    const = lambda i: (0, 0)
    scratch = [pltpu.VMEM((stage_width // LANES, tm, LANES), F32)] if stage_width else []
    return pl.pallas_call(
        functools.partial(_inproj_kernel, plan=plan, n_out=len(out_blocks)),
        grid=(T // tm,),
        in_specs=[
            pl.BlockSpec((tm, D), lambda i: (i, 0)),
            pl.BlockSpec((1, D), const),
            pl.BlockSpec((D, N), const),
            pl.BlockSpec((1, N), const),
            pl.BlockSpec((tm, LANES), lambda i: (i % n_pos_blocks, 0)),
            pl.BlockSpec((tm, LANES), lambda i: (i % n_pos_blocks, 0)),
            pl.BlockSpec((MXU_DIM, MXU_DIM), const),
        ],
        out_specs=[pl.BlockSpec((br, bc), lambda i: (i, 0)) for br, bc in out_blocks],
        out_shape=[jax.ShapeDtypeStruct((T // tm * br, bc), BF16) for br, bc in out_blocks],
        scratch_shapes=scratch,
        compiler_params=_cparams(("parallel",)),
        name="inproj",
    )(x, g, w, hg, cos_t, sin_t, bd)


def _band_window(i, tq, hw, L):
    win = tq + 2 * hw
    start = jnp.clip(i * tq - hw, 0, L - win)
    start = pl.multiple_of(start, 64)
    qpos = i * tq + lax.broadcasted_iota(jnp.int32, (tq, win), 0)
    kpos = start + lax.broadcasted_iota(jnp.int32, (tq, win), 1)
    valid = jnp.abs(kpos - qpos) <= hw
    return start, win, valid


def _band_a_kernel(q_ref, k_ref, v_ref, o_ref, lse_ref, *, hw, tq, L, wb):
    lane = _lane_iota((1, LANES))
    head0 = lane < HEAD_DIM

    def qblock(i, carry):
        start, win, valid = _band_window(i, tq, hw, L)
        rows = pl.ds(pl.multiple_of(i * tq, tq), tq)
        for sl in range(wb // LANES):
            cols = slice(sl * LANES, (sl + 1) * LANES)
            q = q_ref[0, rows, cols]
            kw = k_ref[0, pl.ds(start, win), cols]
            vw = v_ref[0, pl.ds(start, win), cols]
            outs = []
            lses = []
            for hmask in (head0, jnp.logical_not(head0)):
                qh = jnp.where(hmask, q, jnp.zeros_like(q))
                s = lax.dot_general(qh, kw, (((1,), (1,)), ((), ())), preferred_element_type=F32)
                s = jnp.where(valid, s, NEG_INF)
                m = jnp.max(s, axis=-1, keepdims=True)
                p = jnp.exp(s - m)
                den = jnp.sum(p, axis=-1, keepdims=True)
                pv = jnp.dot(p.astype(BF16), vw, preferred_element_type=F32)
                outs.append(pv / den)
                lses.append(m + jnp.log(den))
            o_ref[0, rows, cols] = jnp.where(head0, outs[0], outs[1]).astype(o_ref.dtype)
            lse_ref[0, rows, cols] = jnp.where(head0, lses[0], lses[1])
        return carry

    lax.fori_loop(0, L // tq, qblock, 0, unroll=BAND_UNROLL)


def _band_a(q, k, v, hw):
    NB, L, C = q.shape
    tq = BAND_A_TQ
    assert L >= tq + 2 * hw and L % tq == 0
    wb = min(C, A_HEADS * HEAD_DIM)
    while L * wb * 2 > BAND_BLOCK_BYTES and wb > LANES:
        wb //= 2
    spec = pl.BlockSpec((1, L, wb), lambda n, c: (n, 0, c))
    return pl.pallas_call(
        functools.partial(_band_a_kernel, hw=hw, tq=tq, L=L, wb=wb),
        grid=(NB, C // wb),
        in_specs=[spec, spec, spec],
        out_specs=[spec, spec],
        out_shape=[jax.ShapeDtypeStruct((NB, L, C), BF16), jax.ShapeDtypeStruct((NB, L, C), F32)],
        compiler_params=_cparams(("parallel", "parallel")),
        name="band_a",
    )(q, k, v)


def _band_core(i, L, hw, q_heads, k_ref, kcols, vT_scr, vslab, sink_row):
    tq = min(BAND_TQ, L)
    kb = BAND_KBLK
    nblk = L // kb
    wblk = min(tq // kb + 2, nblk)
    win = wblk * kb
    blk0 = jnp.clip(i * (tq // kb) - 1, 0, nblk - wblk)
    kw = k_ref[0, pl.ds(pl.multiple_of(blk0 * kb, kb), win), kcols]
    qq = jnp.concatenate(q_heads, axis=0)
    s = lax.dot_general(kw, qq, (((1,), (1,)), ((), ())), preferred_element_type=F32)
    kpos = blk0 * kb + lax.broadcasted_iota(jnp.int32, (win, tq), 0)
    qpos = i * tq + lax.broadcasted_iota(jnp.int32, (win, tq), 1)
    valid = jnp.abs(kpos - qpos) <= hw
    s = jnp.where(jnp.concatenate([valid] * len(q_heads), axis=1), s, NEG_INF)
    m = jnp.max(s, axis=0, keepdims=True)
    if sink_row is not None:
        m = jnp.maximum(m, sink_row)
    p = jnp.exp(s - m).astype(BF16)
    vT = jnp.concatenate([vT_scr[vslab, blk0 + w] for w in range(wblk)], axis=1)
    lhs = jnp.concatenate([vT, jnp.ones((BAND_ONES_ROWS, win), BF16)], axis=0)
    return jnp.dot(lhs, p, preferred_element_type=F32), m


def _transpose_v(v_ref, vT_scr, L, n_slabs):
    def blk_step(blk, carry):
        rows = pl.ds(pl.multiple_of(blk * BAND_KBLK, BAND_KBLK), BAND_KBLK)
        for sl in range(n_slabs):
            vb = v_ref[0, rows, sl * LANES:(sl + 1) * LANES].astype(F32)
            vT_scr[sl, blk] = vb.T.astype(BF16)
        return carry
    lax.fori_loop(0, L // BAND_KBLK, blk_step, 0)


def _band_b_kernel(sink_ref, q_ref, k_ref, v_ref, o_ref, vT_scr, *, hw, L):
    tq = min(BAND_TQ, L)
    lane = _lane_iota((1, LANES))
    head0 = lane < HEAD_DIM
    row0 = lax.broadcasted_iota(jnp.int32, (LANES, 1), 0) < HEAD_DIM
    hk = pl.program_id(1)
    group = B_Q_HEADS // B_KV_HEADS
    sink = jnp.concatenate(
        [jnp.full((1, tq), sink_ref[hk * group + g], F32) for g in range(group)], axis=1)
    _transpose_v(v_ref, vT_scr, L, 1)

    def qblock(i, carry):
        rows = pl.ds(pl.multiple_of(i * tq, tq), tq)
        qs = []
        for g in range(group):
            q = q_ref[0, rows, (g // 2) * LANES:(g // 2 + 1) * LANES]
            zero = jnp.zeros_like(q)
            qs.append(jnp.where(head0, q, zero) if g % 2 == 0 else jnp.where(head0, zero, q))
        acc, m = _band_core(i, L, hw, qs, k_ref, slice(0, LANES), vT_scr, 0, sink)
        den = acc[LANES:LANES + 1] + jnp.exp(sink - m)
        oT = acc[:LANES] / den
        for pair in range(group // 2):
            a = oT[:, (2 * pair) * tq:(2 * pair + 1) * tq]
            b = oT[:, (2 * pair + 1) * tq:(2 * pair + 2) * tq]
            o_ref[0, rows, pair * LANES:(pair + 1) * LANES] = jnp.where(row0, a, b).T.astype(o_ref.dtype)
        return carry

    lax.fori_loop(0, L // tq, qblock, 0, unroll=min(BAND_UNROLL, L // tq))


def _band_b(sink, q, kdup, vdup):
    B, S, C = q.shape
    tq = BAND_TQ
    hw = B_HALF_WINDOW
    qw = C // B_KV_HEADS
    assert hw <= BAND_KBLK and S % tq == 0
    return pl.pallas_call(
        functools.partial(_band_b_kernel, hw=hw, L=S),
        grid=(B, B_KV_HEADS),
        in_specs=[
            pl.BlockSpec(memory_space=pltpu.SMEM),
            pl.BlockSpec((1, S, qw), lambda b, h: (b, 0, h)),
            pl.BlockSpec((1, S, LANES), lambda b, h: (b, 0, h)),
            pl.BlockSpec((1, S, LANES), lambda b, h: (b, 0, h)),
        ],
        out_specs=pl.BlockSpec((1, S, qw), lambda b, h: (b, 0, h)),
        out_shape=jax.ShapeDtypeStruct((B, S, C), BF16),
        scratch_shapes=[pltpu.VMEM((1, S // BAND_KBLK, LANES, BAND_KBLK), BF16)],
        compiler_params=_cparams(("parallel", "parallel")),
        name="band_b",
    )(sink, q, kdup, vdup)


def _outproj_even_kernel(o1_ref, o4_ref, o16_ref, l1_ref, l4_ref, l16_ref, ob_ref, w_ref, x_ref, y_ref,
                         so4, sl4, so16, sl16):
    tm, half = o1_ref.shape

    def token_major(src_ref, stage_ref, d):
        for rr in range(d):
            for j in range(half // LANES):
                stage_ref[j, pl.ds(rr, tm // d, stride=d), :] = (
                    src_ref[:, rr * half + j * LANES:rr * half + (j + 1) * LANES].astype(F32))
        return jnp.concatenate([stage_ref[j] for j in range(half // LANES)], axis=1)

    dils = [d for _, d in DILATION_GROUPS]
    l1 = l1_ref[...]
    o4, l4 = token_major(o4_ref, so4, dils[1]), token_major(l4_ref, sl4, dils[1])
    o16, l16 = token_major(o16_ref, so16, dils[2]), token_major(l16_ref, sl16, dils[2])
    m = jnp.maximum(jnp.maximum(l1, l4), l16)
    e1, e4, e16 = jnp.exp(l1 - m), jnp.exp(l4 - m), jnp.exp(l16 - m)
    num = e1 * o1_ref[...].astype(F32) + e4 * o4 + e16 * o16
    oa = (num / (e1 + e4 + e16)).astype(BF16)
    ka = oa.shape[1]
    acc = jnp.dot(oa, w_ref[:ka, :], preferred_element_type=F32)
    acc = acc + jnp.dot(ob_ref[...], w_ref[ka:, :], preferred_element_type=F32)
    y_ref[...] = x_ref[...] + acc


def _outproj_even(o1, o4, o16, l1, l4, l16, ob, w, x):
    T, D = x.shape
    tm = ROW_TILE
    half = o1.shape[1]
    row = lambda wd: pl.BlockSpec((tm, wd), lambda i: (i, 0))
    view = lambda d: pl.BlockSpec((tm // d, d * half), lambda i: (i, 0))
    d4, d16 = DILATION_GROUPS[1][1], DILATION_GROUPS[2][1]
    return pl.pallas_call(
        _outproj_even_kernel,
        grid=(T // tm,),
        in_specs=[row(half), view(d4), view(d16), row(half), view(d4), view(d16), row(half),
                  pl.BlockSpec(w.shape, lambda i: (0, 0)), row(D)],
        out_specs=row(D),
        out_shape=jax.ShapeDtypeStruct((T, D), F32),
        scratch_shapes=[pltpu.VMEM((half // LANES, tm, LANES), F32)] * 4,
        compiler_params=_cparams(("parallel",)),
        name="outproj_even",
    )(o1, o4, o16, l1, l4, l16, ob, w, x)


def _outproj_kernel(o_ref, w_ref, x_ref, y_ref):
    y_ref[...] = x_ref[...] + jnp.dot(o_ref[...], w_ref[...], preferred_element_type=F32)


def _outproj(o, w, x):
    T, D = x.shape
    tm = ROW_TILE
    row = lambda wd: pl.BlockSpec((tm, wd), lambda i: (i, 0))
    return pl.pallas_call(
        _outproj_kernel,
        grid=(T // tm,),
        in_specs=[row(o.shape[1]), pl.BlockSpec(w.shape, lambda i: (0, 0)), row(D)],
        out_specs=row(D),
        out_shape=jax.ShapeDtypeStruct((T, D), F32),
        compiler_params=_cparams(("parallel",)),
        name="outproj",
    )(o, w, x)


def _diff_kernel(lam_ref, qT_ref, k_ref, vT_ref, sub_ref, o_ref, s0_scr, s1_scr, acc_scr, *,
                 tq, tk, S, lam_init, unroll):
    row = lax.broadcasted_iota(jnp.int32, (LANES, 1), 0)
    comp0 = row < HEAD_DIM
    lamv = lam_ref[...]
    lam = (jnp.exp(jnp.sum(lamv[0:1] * lamv[1:2], axis=-1, keepdims=True))
           - jnp.exp(jnp.sum(lamv[2:3] * lamv[3:4], axis=-1, keepdims=True)) + lam_init)
    qT = qT_ref[0, 0]
    zero = jnp.zeros_like(qT)
    qqT = jnp.concatenate([jnp.where(comp0, qT, zero), jnp.where(comp0, zero, qT)], axis=1)
    ones = jnp.ones((DIFF_ONES_ROWS, tk), BF16)
    n_kv = S // tk

    def scores(t, dst):
        kb = k_ref[0, pl.ds(pl.multiple_of(t * tk, tk), tk), :]
        dst[...] = jnp.dot(kb, qqT, preferred_element_type=F32)

    def softmax_pv(src, t, m):
        s = src[...]
        lhs = jnp.concatenate([vT_ref[0, 0, t], ones], axis=0)
        m_new = jnp.maximum(m, jnp.max(s, axis=0, keepdims=True))
        alpha = jnp.exp2(m - m_new)
        p = jnp.exp2(s - m_new).astype(BF16)
        acc_scr[...] = alpha * acc_scr[...] + jnp.dot(lhs, p, preferred_element_type=F32)
        return m_new

    def step(i, m):
        bufs = (s0_scr, s1_scr)
        for u in range(unroll):
            t = unroll * i + u
            scores(jnp.minimum(t + 1, n_kv - 1), bufs[(u + 1) % 2])
            m = softmax_pv(bufs[u % 2], t, m)
        return m

    acc_scr[...] = jnp.zeros_like(acc_scr)
    scores(0, s0_scr)
    lax.fori_loop(0, n_kv // unroll, step, jnp.full((1, 2 * tq), NEG_INF, F32))
    acc = acc_scr[...]
    a = acc[:LANES] / acc[LANES:LANES + 1]
    o = a[:, :tq] - lam * a[:, tq:]
    ms = jnp.mean(o * o, axis=0, keepdims=True)
    o = o * lax.rsqrt(ms + EPS) * sub_ref[...] * (1.0 - lam_init)
    o_ref[0] = o.T.astype(o_ref.dtype)


def _diff_attention(lam_vecs, qT, k, vT, subln_col, lam_init):
    B, H, _, S = qT.shape
    tq, tk = DIFF_TQ, DIFF_TK
    unroll = min(DIFF_UNROLL, S // tk)
    assert S % tq == 0 and S % (tk * unroll) == 0 and unroll % 2 == 0
    acc_rows = LANES + DIFF_ONES_ROWS
    return pl.pallas_call(
        functools.partial(_diff_kernel, tq=tq, tk=tk, S=S, lam_init=lam_init, unroll=unroll),
        grid=(B, H, S // tq),
        in_specs=[
            pl.BlockSpec(lam_vecs.shape, lambda b, h, i: (0, 0)),
            pl.BlockSpec((1, 1, LANES, tq), lambda b, h, i: (b, h, 0, i)),
            pl.BlockSpec((1, S, LANES), lambda b, h, i: (b, 0, h)),
            pl.BlockSpec((1, 1, S // tk, LANES, tk), lambda b, h, i: (b, h, 0, 0, 0)),
            pl.BlockSpec((LANES, 1), lambda b, h, i: (0, 0)),
        ],
        out_specs=pl.BlockSpec((1, tq, LANES), lambda b, h, i: (b, i, h)),
        out_shape=jax.ShapeDtypeStruct((B, S, H * LANES), BF16),
        scratch_shapes=[pltpu.VMEM((tk, 2 * tq), F32), pltpu.VMEM((tk, 2 * tq), F32),
                        pltpu.VMEM((acc_rows, 2 * tq), F32)],
        compiler_params=_cparams(("parallel", "parallel", "arbitrary")),
        name="diff_attn",
    )(lam_vecs, qT, k, vT, subln_col)


def _router_kernel(x_ref, g_ref, whi_ref, wlo_ref, b_ref, xn_ref, gate_ref, eid_ref, cnt_ref, *, n_token_tiles):
    i = pl.program_id(0)

    @pl.when(i < n_token_tiles)
    def _():
        _router_tile(x_ref, g_ref, whi_ref, wlo_ref, b_ref, xn_ref, gate_ref, eid_ref, cnt_ref)

    @pl.when(i >= n_token_tiles)
    def _():
        xn_ref[...] = jnp.zeros_like(xn_ref)


def _router_tile(x_ref, g_ref, whi_ref, wlo_ref, b_ref, xn_ref, gate_ref, eid_ref, cnt_ref):
    x = x_ref[...]
    ms = jnp.mean(x * x, axis=-1, keepdims=True)
    xn = x * lax.rsqrt(ms + EPS) * g_ref[...]
    xhi = xn.astype(BF16)
    half = xn.shape[1] // 2
    hi_bits = lax.bitcast_convert_type(xhi[:, :half].astype(F32), jnp.uint32)
    lo_bits = lax.bitcast_convert_type(xhi[:, half:].astype(F32), jnp.uint32)
    xn_ref[...] = hi_bits | (lo_bits >> 16)
    xlo = (xn - xhi.astype(F32)).astype(BF16)
    whi = whi_ref[...]
    logits = (jnp.dot(xhi, whi, preferred_element_type=F32)
              + jnp.dot(xlo, whi, preferred_element_type=F32)
              + jnp.dot(xhi, wlo_ref[...], preferred_element_type=F32)) + b_ref[...]
    lane = _lane_iota(logits.shape).astype(F32)
    big = float(4 * LANES)
    lg = jnp.where(lane < N_GROUPS, logits, NEG_INF)
    mg = jnp.max(lg, axis=-1, keepdims=True)
    g_sel = jnp.min(jnp.where(lg == mg, lane, big), axis=-1, keepdims=True)
    p_sel = 1.0 / jnp.sum(jnp.exp(lg - mg), axis=-1, keepdims=True)
    e_lo = ROUTER_E0 + g_sel * EXPERTS_PER_GROUP
    le = jnp.where((lane >= e_lo) & (lane < e_lo + EXPERTS_PER_GROUP), logits, NEG_INF)
    v1 = jnp.max(le, axis=-1, keepdims=True)
    i1 = jnp.min(jnp.where(le == v1, lane, big), axis=-1, keepdims=True)
    le2 = jnp.where(lane == i1, NEG_INF, le)
    v2 = jnp.max(le2, axis=-1, keepdims=True)
    i2 = jnp.min(jnp.where(le2 == v2, lane, big), axis=-1, keepdims=True)
    e2 = jnp.exp(v2 - v1)
    den = 1.0 + e2
    g1 = (1.0 / den) * p_sel
    g2 = (e2 / den) * p_sel
    gate_ref[...] = jnp.where(lane == 0, g1, jnp.where(lane == 1, g2, 0.0))
    eid = jnp.where(lane == 0, i1 - ROUTER_E0, jnp.where(lane == 1, i2 - ROUTER_E0, 0.0))
    eid_ref[...] = eid.astype(jnp.int32)
    chosen = ((lane == i1) | (lane == i2)).astype(F32)

    @pl.when(pl.program_id(0) == 0)
    def _():
        cnt_ref[...] = jnp.zeros_like(cnt_ref)

    cnt_ref[...] += jnp.sum(chosen, axis=0, keepdims=True)


def _router(x, g, whi, wlo, bias):
    T, D = x.shape
    tm = ROW_TILE
    n_tok = T // tm
    table_rows = max(T, DISPATCH_TABLE_ROWS)
    const = lambda i: (0, 0)
    row = lambda wd: pl.BlockSpec((tm, wd), lambda i: (jnp.minimum(i, n_tok - 1), 0))
    return pl.pallas_call(
        functools.partial(_router_kernel, n_token_tiles=n_tok),
        grid=(table_rows // tm,),
        in_specs=[row(D), pl.BlockSpec((1, D), const), pl.BlockSpec((D, ROUTER_COLS), const),
                  pl.BlockSpec((D, ROUTER_COLS), const), pl.BlockSpec((1, ROUTER_COLS), const)],
        out_specs=[pl.BlockSpec((tm, D // 2), lambda i: (i, 0)), row(ROUTER_COLS), row(ROUTER_COLS),
                   pl.BlockSpec((1, ROUTER_COLS), const)],
        out_shape=[jax.ShapeDtypeStruct((table_rows, D // 2), jnp.uint32),
                   jax.ShapeDtypeStruct((T, ROUTER_COLS), F32),
                   jax.ShapeDtypeStruct((T, ROUTER_COLS), jnp.int32),
                   jax.ShapeDtypeStruct((1, ROUTER_COLS), F32)],
        compiler_params=_cparams(("arbitrary",)),
        name="router",
    )(x, g, whi, wlo, bias)


def _slot_kernel(eid_ref, pstart_ref, slot_ref, run_ref):
    tm = eid_ref.shape[0]

    @pl.when(pl.program_id(0) == 0)
    def _():
        run_ref[...] = jnp.zeros_like(run_ref)

    eid = eid_ref[...]
    lane = _lane_iota(eid.shape)
    oh0 = lane == (eid[:, 0:1] + ROUTER_E0)
    oh1 = lane == (eid[:, 1:2] + ROUTER_E0)
    both = (oh0 | oh1).astype(BF16)
    r = lax.broadcasted_iota(jnp.int32, (tm, tm), 0)
    c = lax.broadcasted_iota(jnp.int32, (tm, tm), 1)
    earlier = (c < r).astype(BF16)
    before = jnp.dot(earlier, both, preferred_element_type=F32)
    base = pstart_ref[...] + run_ref[...] + before
    s0 = jnp.sum(jnp.where(oh0, base, 0.0), axis=-1, keepdims=True)
    s1 = jnp.sum(jnp.where(oh1, base, 0.0), axis=-1, keepdims=True)
    slot_ref[...] = jnp.where(lane == 0, s0, jnp.where(lane == 1, s1, 0.0)).astype(jnp.int32)
    run_ref[...] += jnp.sum(both.astype(F32), axis=0, keepdims=True)


def _slots(eid, pstart_row):
    T = eid.shape[0]
    tm = ROW_TILE
    return pl.pallas_call(
        _slot_kernel,
        grid=(T // tm,),
        in_specs=[pl.BlockSpec((tm, ROUTER_COLS), lambda i: (i, 0)),
                  pl.BlockSpec((1, ROUTER_COLS), lambda i: (0, 0))],
        out_specs=pl.BlockSpec((tm, ROUTER_COLS), lambda i: (i, 0)),
        out_shape=jax.ShapeDtypeStruct((T, ROUTER_COLS), jnp.int32),
        scratch_shapes=[pltpu.VMEM((1, ROUTER_COLS), F32)],
        compiler_params=_cparams(("arbitrary",)),
        name="slots",
    )(eid, pstart_row)


def _expert_kernel(te_ref, nt_ref, nxt_ref, par_ref, xs_ref, wg_hbm, wu_hbm, wd_hbm, y_ref,
                   wg_f, wu_f, wd_f, wg_s, wu_s, wd_s, sem, *, layer):
    t = pl.program_id(0)
    live = t < nt_ref[0]
    e = te_ref[t]
    new_expert = (t == 0) | (e != te_ref[jnp.maximum(t - 1, 0)])
    slot = par_ref[t]

    def weight_copies(expert, s):
        return (pltpu.make_async_copy(wg_hbm.at[layer, expert], wg_f.at[s], sem.at[s, 0]),
                pltpu.make_async_copy(wu_hbm.at[layer, expert], wu_f.at[s], sem.at[s, 1]),
                pltpu.make_async_copy(wd_hbm.at[layer, expert], wd_f.at[s], sem.at[s, 2]))

    @pl.when(live & (t == 0))
    def _():
        for c in weight_copies(e, slot):
            c.start()

    @pl.when(live & new_expert)
    def _():
        for c in weight_copies(e, slot):
            c.wait()
        wg_s[...] = wg_f[slot].astype(BF16)
        wu_s[...] = wu_f[slot].astype(BF16)
        wd_s[...] = wd_f[slot].astype(BF16)
        nxt = nxt_ref[t]

        @pl.when(nxt >= 0)
        def _():
            for c in weight_copies(nxt, 1 - slot):
                c.start()

    @pl.when(live)
    def _():
        words = xs_ref[...]
        hi = lax.bitcast_convert_type(words & jnp.uint32(0xFFFF0000), F32).astype(BF16)
        lo = lax.bitcast_convert_type(words << 16, F32).astype(BF16)
        xs = jnp.concatenate([hi, lo], axis=1)
        a = jnp.dot(xs, wg_s[...], preferred_element_type=F32)
        u = jnp.dot(xs, wu_s[...], preferred_element_type=F32)
        h = (a * jax.nn.sigmoid(a) * u).astype(BF16)
        y_ref[...] = jnp.dot(h, wd_s[...], preferred_element_type=F32).astype(y_ref.dtype)

    @pl.when(jnp.logical_not(live))
    def _():
        y_ref[...] = jnp.zeros_like(y_ref)


def _experts(tile_expert, n_tiles, next_expert, parity, xs, wg, wu, wd, layer):
    Mp = xs.shape[0]
    D = wg.shape[2]
    te = EXPERT_TILE
    grid_spec = pltpu.PrefetchScalarGridSpec(
        num_scalar_prefetch=4,
        grid=(Mp // te,),
        in_specs=[
            pl.BlockSpec((te, D // 2), lambda t, *_: (t, 0)),
            pl.BlockSpec(memory_space=pl.ANY),
            pl.BlockSpec(memory_space=pl.ANY),
            pl.BlockSpec(memory_space=pl.ANY),
        ],
        out_specs=pl.BlockSpec((te, D), lambda t, *_: (t, 0)),
        scratch_shapes=[pltpu.VMEM((2, D, D_EXPERT), F32), pltpu.VMEM((2, D, D_EXPERT), F32),
                        pltpu.VMEM((2, D_EXPERT, D), F32),
                        pltpu.VMEM((D, D_EXPERT), BF16), pltpu.VMEM((D, D_EXPERT), BF16),
                        pltpu.VMEM((D_EXPERT, D), BF16),
                        pltpu.SemaphoreType.DMA((2, 3))],
    )
    return pl.pallas_call(
        functools.partial(_expert_kernel, layer=layer),
        grid_spec=grid_spec,
        out_shape=jax.ShapeDtypeStruct((Mp, D), BF16),
        compiler_params=_cparams(("arbitrary",)),
        name="experts",
    )(tile_expert, n_tiles, next_expert, parity, xs, wg, wu, wd)


def _combine_kernel(x_ref, y0_ref, y1_ref, gate_ref, o_ref):
    gate = gate_ref[...]
    o_ref[...] = (x_ref[...] + gate[:, 0:1] * y0_ref[...].astype(F32)
                  + gate[:, 1:2] * y1_ref[...].astype(F32))


def _combine(x, y0, y1, gate):
    T, D = x.shape
    tm = ROW_TILE
    row = lambda wd: pl.BlockSpec((tm, wd), lambda i: (i, 0))
    return pl.pallas_call(
        _combine_kernel,
        grid=(T // tm,),
        in_specs=[row(D), row(D), row(D), row(ROUTER_COLS)],
        out_specs=row(D),
        out_shape=jax.ShapeDtypeStruct((T, D), F32),
        compiler_params=_cparams(("parallel",)),
        name="moe_combine",
    )(x, y0, y1, gate)


def _rows(table, idx):
    return table.at[idx].get(mode="promise_in_bounds")


def _moe_route(x, p):
    T, D = x.shape
    xn, gate, eid, counts_row = _router(x, p["ffn_norm"], p["r_whi"], p["r_wlo"], p["r_bias"])
    M = T * TOP_K
    te = EXPERT_TILE
    Mp = M + N_EXPERTS * te
    counts = counts_row[0, ROUTER_E0:ROUTER_E0 + N_EXPERTS].astype(jnp.int32)
    padded = ((counts + te - 1) // te) * te
    pend = jnp.cumsum(padded)
    pstart = pend - padded
    pstart_row = jnp.zeros((1, ROUTER_COLS), F32).at[0, ROUTER_E0:ROUTER_E0 + N_EXPERTS].set(pstart.astype(F32))
    slot = _slots(eid, pstart_row)[:, :TOP_K]
    tok_of_slot = jnp.zeros((Mp,), jnp.int32).at[slot.reshape(M)].set(
        jnp.arange(M, dtype=jnp.int32) // TOP_K, mode="promise_in_bounds", unique_indices=True)
    tile_ids = jnp.arange(Mp // te, dtype=jnp.int32)
    tile_expert = jnp.minimum(jnp.sum((pend // te)[None, :] <= tile_ids[:, None], axis=1),
                              N_EXPERTS - 1).astype(jnp.int32)
    n_tiles = (pend[-1] // te).astype(jnp.int32).reshape(1)
    present = counts > 0
    ids = jnp.arange(N_EXPERTS, dtype=jnp.int32)
    later = (ids[None, :] > ids[:, None]) & present[None, :]
    next_present = jnp.min(jnp.where(later, ids[None, :], N_EXPERTS), axis=1)
    next_present = jnp.where(next_present == N_EXPERTS, -1, next_present).astype(jnp.int32)
    rank = (jnp.cumsum(present.astype(jnp.int32)) - present.astype(jnp.int32))
    return dict(x=x, xn=xn, gate=gate, slot=slot, tok_of_slot=tok_of_slot, tile_expert=tile_expert,
                n_tiles=n_tiles, next_expert=next_present[tile_expert],
                parity=(rank[tile_expert] % 2).astype(jnp.int32))


def _moe_experts(r, xs, p, layer):
    return _experts(r["tile_expert"], r["n_tiles"], r["next_expert"], r["parity"], xs,
                    p["w_gate"], p["w_up"], p["w_down"], layer)


def _moe(x, p, layer):
    r = _moe_route(x, p)
    ys = _moe_experts(r, _rows(r["xn"], r["tok_of_slot"]), p, layer)
    return _combine(x, _rows(ys, r["slot"][:, 0]), _rows(ys, r["slot"][:, 1]), r["gate"])


def _rope_tables(seq):
    half = HEAD_DIM // 2
    inv = ROPE_THETA ** (-jnp.arange(half, dtype=F32) / half)
    ang = jnp.arange(seq, dtype=F32)[:, None] * inv[None, :]
    cos = jnp.tile(jnp.cos(ang), (1, LANES // half))
    sin = jnp.sin(ang)
    sin = jnp.tile(jnp.concatenate([-sin, sin], axis=1), (1, LANES // HEAD_DIM))
    return cos, sin


def _head_gain(g, heads):
    return jnp.tile(g.astype(F32), (heads,))


def _prepare(ev_norm, ev_w_in, ev_qn_a, ev_kn_a, ev_qn_b, ev_kn_b, ev_sink, ev_w_out,
             od_norm, od_w_in, od_qn, od_kn, od_lq1, od_lk1, od_lq2, od_lk2, od_subln, od_w_out,
             ffn_norm, rg_w, rg_b, re_w, re_b, w_gate, w_up, w_down):
    hd = HEAD_DIM
    P = {}
    r = jnp.arange(MXU_DIM) // hd
    P["bd"] = (r[:, None] == r[None, :]).astype(BF16)
    w = ev_w_in[0]
    a3 = 3 * A_HEADS * hd
    qb_end = a3 + B_Q_HEADS * hd
    kb = [w[:, qb_end + h * hd:qb_end + (h + 1) * hd] for h in range(B_KV_HEADS)]
    vb0 = qb_end + B_KV_HEADS * hd
    vb = [w[:, vb0 + h * hd:vb0 + (h + 1) * hd] for h in range(B_KV_HEADS)]
    P["ev_w_in"] = jnp.concatenate([w[:, :qb_end]] + [c for h in kb for c in (h, h)]
                                   + [c for h in vb for c in (h, h)], axis=1).astype(BF16)
    ones = jnp.ones((A_HEADS * hd,), F32)
    P["ev_hg"] = jnp.concatenate([
        _head_gain(ev_qn_a[0], A_HEADS), _head_gain(ev_kn_a[0], A_HEADS), ones,
        _head_gain(ev_qn_b[0], B_Q_HEADS), _head_gain(ev_kn_b[0], 2 * B_KV_HEADS),
        jnp.ones((2 * B_KV_HEADS * hd,), F32)])[None, :]
    P["ev_norm"] = ev_norm[0][None, :]
    P["ev_sink"] = ev_sink[0].astype(F32)
    P["ev_w_out"] = ev_w_out[0].astype(BF16)
    P["od_w_in"] = od_w_in[0].astype(BF16)
    P["od_hg"] = jnp.concatenate([_head_gain(od_qn[0], 2 * C_HEADS), _head_gain(od_kn[0], 2 * C_HEADS),
                                  jnp.ones((2 * C_HEADS * hd,), F32)])[None, :]
    P["od_norm"] = od_norm[0][None, :]
    P["od_lam"] = jnp.stack([od_lq1[0], od_lk1[0], od_lq2[0], od_lk2[0]]).astype(F32)
    P["od_subln"] = od_subln[0][:, None].astype(F32)
    P["od_w_out"] = od_w_out[0].astype(BF16)
    P["moe"] = []
    for layer in range(ffn_norm.shape[0]):
        wr = jnp.zeros((D_MODEL, ROUTER_COLS), F32)
        wr = wr.at[:, :N_GROUPS].set(rg_w[layer])
        we = jnp.transpose(re_w[layer], (1, 0, 2)).reshape(D_MODEL, N_EXPERTS)
        wr = wr.at[:, ROUTER_E0:ROUTER_E0 + N_EXPERTS].set(we)
        bias = jnp.zeros((ROUTER_COLS,), F32).at[:N_GROUPS].set(rg_b[layer])
        bias = bias.at[ROUTER_E0:ROUTER_E0 + N_EXPERTS].set(re_b[layer].reshape(N_EXPERTS))
        whi = wr.astype(BF16)
        P["moe"].append({
            "ffn_norm": ffn_norm[layer][None, :],
            "r_whi": whi, "r_wlo": (wr - whi.astype(F32)).astype(BF16), "r_bias": bias[None, :],
            "w_gate": w_gate, "w_up": w_up, "w_down": w_down,
        })
    return P


def _even_mixer(x, P, B, S):
    T = B * S
    hd = HEAD_DIM
    aw = A_HEADS * hd
    cos_t, sin_t = _rope_tables(S)
    scale = hd ** -0.5
    tm = ROW_TILE
    dils = [d for _, d in DILATION_GROUPS]
    assert dils[0] == 1
    out_blocks = [(tm, aw)] * 4 + [(tm, 2 * LANES)] * 2
    views = []
    for _ in range(3):
        tv = []
        for d in dils[1:]:
            tv.append((d, len(out_blocks)))
            out_blocks.append((tm // d, d * aw))
        views.append(tuple(tv))
    plan = ((0, 0, aw, True, scale, views[0]), (1, aw, aw, True, 1.0, views[1]),
            (2, 2 * aw, aw, False, 1.0, views[2]), (3, 3 * aw, aw, True, scale, ()),
            (4, 4 * aw, 2 * LANES, True, 1.0, ()), (5, 4 * aw + 2 * LANES, 2 * LANES, False, 1.0, ()))
    res = _inproj(x, P["ev_norm"], P["ev_w_in"], P["ev_hg"], cos_t, sin_t, P["bd"], plan, out_blocks, S,
                  stage_width=aw)
    qb, kb, vb = res[3:6]
    outs, lses = [], []
    for gi, (window, dil) in enumerate(DILATION_GROUPS):
        L = S // dil
        qkv = [res[t] if dil == 1 else res[views[t][gi - 1][1]] for t in range(3)]
        o, lse = _band_a(*[t.reshape(B, L, dil * aw) for t in qkv], window // (2 * dil))
        outs.append(o.reshape(T // dil, dil * aw))
        lses.append(lse.reshape(T // dil, dil * aw))
    ob = _band_b(P["ev_sink"], qb.reshape(B, S, aw), kb.reshape(B, S, 2 * LANES), vb.reshape(B, S, 2 * LANES))
    return _outproj_even(outs[0], outs[1], outs[2], lses[0], lses[1], lses[2], ob.reshape(T, aw),
                         P["ev_w_out"], x)


def _diff_mixer(x, P, B, S, lam_init):
    T = B * S
    hd = HEAD_DIM
    cw = 2 * C_HEADS * hd
    cos_t, sin_t = _rope_tables(S)
    plan = ((0, 0, cw, True, LOG2_E * hd ** -0.5, ()), (1, cw, cw, True, 1.0, ()), (2, 2 * cw, cw, False, 1.0, ()))
    q, k, v = _inproj(x, P["od_norm"], P["od_w_in"], P["od_hg"], cos_t, sin_t, P["bd"], plan,
                      [(ROW_TILE, cw)] * 3, S)
    tk = DIFF_TK
    qT = q.reshape(B, S, C_HEADS, LANES).transpose(0, 2, 3, 1)
    vT = v.reshape(B, S // tk, tk, C_HEADS, LANES).transpose(0, 3, 1, 4, 2)
    o = _diff_attention(P["od_lam"], qT, k.reshape(B, S, cw), vT, P["od_subln"], lam_init)
    return _outproj(o.reshape(T, cw), P["od_w_out"], x)


def _flat(x3):
    B, S, D = x3.shape
    return x3.reshape(B * S, D)


def kernel(x_prompt, x_sample, ev_norm, ev_w_in, ev_qn_a, ev_kn_a, ev_qn_b, ev_kn_b, ev_sink, ev_w_out,
           od_norm, od_w_in, od_qn, od_kn, od_lq1, od_lk1, od_lq2, od_lk2, od_subln, od_w_out,
           ffn_norm, rg_w, rg_b, re_w, re_b, w_gate, w_up, w_down):
    P = _prepare(ev_norm, ev_w_in, ev_qn_a, ev_kn_a, ev_qn_b, ev_kn_b, ev_sink, ev_w_out,
                 od_norm, od_w_in, od_qn, od_kn, od_lq1, od_lk1, od_lq2, od_lk2, od_subln, od_w_out,
                 ffn_norm, rg_w, rg_b, re_w, re_b, w_gate, w_up, w_down)
    lam_init = 0.8 - 0.6 * math.exp(-0.3 * 1)
    (Ba, Sa, D), (Bb, Sb, _) = x_prompt.shape, x_sample.shape
    moe = P["moe"]

    def tie(first, second):
        return lax.optimization_barrier((first, second))

    def moe_overlapped(r, layer, big_other, route_other):
        xs = _rows(r["xn"], r["tok_of_slot"])
        other_x = big_other()
        xs, other_x = tie(xs, other_x)
        ys = _moe_experts(r, xs, moe[layer], layer)
        y0, y1 = _rows(ys, r["slot"][:, 0]), _rows(ys, r["slot"][:, 1])
        other_r = route_other(other_x)
        if other_r is not None:
            (y0, y1), other_r = tie((y0, y1), other_r)
        return _combine(r["x"], y0, y1, r["gate"]), other_r

    xa = _even_mixer(_flat(x_prompt), P, Ba, Sa)
    ra = _moe_route(xa, moe[0])
    xa, rb = moe_overlapped(ra, 0, lambda: _even_mixer(_flat(x_sample), P, Bb, Sb),
                            lambda xb: _moe_route(xb, moe[0]))
    xb, ra = moe_overlapped(rb, 0, lambda: _diff_mixer(xa, P, Ba, Sa, lam_init),
                            lambda xa1: _moe_route(xa1, moe[1]))
    xa, rb = moe_overlapped(ra, 1, lambda: _diff_mixer(xb, P, Bb, Sb, lam_init),
                            lambda xb1: _moe_route(xb1, moe[1]))
    xb, _ = moe_overlapped(rb, 1, lambda: xa, lambda _x: None)
    return (xa.reshape(Ba, Sa, D), xb.reshape(Bb, Sb, D))
```

```python
import functools
import math

import jax
import jax.numpy as jnp
from jax import lax
from jax.experimental import pallas as pl
from jax.experimental.pallas import tpu as pltpu

F32 = jnp.float32
BF16 = jnp.bfloat16

D_MODEL = 1024
HEAD_DIM = 64
A_HEADS = 8
DILATION_GROUPS = ((128, 1), (512, 4), (2048, 16))
B_Q_HEADS = 8
B_KV_HEADS = 2
B_HALF_WINDOW = 128
C_HEADS = 8
N_GROUPS = 4
EXPERTS_PER_GROUP = 8
N_EXPERTS = N_GROUPS * EXPERTS_PER_GROUP
TOP_K = 2
D_EXPERT = 512
ROPE_THETA = 10000.0
EPS = 1e-6
NEG_INF = -1e30

LANES = 128
MXU_DIM = 256
VMEM_LIMIT_BYTES = 52 * 1024 * 1024

ROW_TILE = 512
COL_CHUNK = 512
BAND_A_TQ = 128
BAND_TQ = 256
BAND_KBLK = 128
BAND_ONES_ROWS = 16
BAND_A_UNROLL = 8
BAND_UNROLL = 4
BAND_BLOCK_BYTES =2 * 1024 * 1024
DIFF_TQ = 512
DIFF_TK = 256
DIFF_UNROLL = 16
DIFF_ONES_ROWS = 16
LOG2_E = 1.4426950408889634
EXPERT_TILE = 256
ROUTER_COLS = 128
ROUTER_E0 = 8
DISPATCH_TABLE_ROWS = 32768


def _cparams(sem):
    return pltpu.CompilerParams(dimension_semantics=sem, vmem_limit_bytes=VMEM_LIMIT_BYTES)


def _lane_iota(shape):
    return lax.broadcasted_iota(jnp.int32, shape, len(shape) - 1)


def _inproj_kernel(x_ref, g_ref, w_ref, hg_ref, cos_ref, sin_ref, bd_ref, *refs, plan, n_out):
    out_refs = refs[:n_out]
    stage_ref = refs[n_out] if len(refs) > n_out else None
    x = x_ref[...]
    tm = x.shape[0]
    ms = jnp.mean(x * x, axis=-1, keepdims=True)
    xn = (x * lax.rsqrt(ms + EPS) * g_ref[...]).astype(BF16)
    lane = _lane_iota((1, LANES))
    first_half = (lane % HEAD_DIM) < (HEAD_DIM // 2)
    cos = cos_ref[...]
    sin = sin_ref[...]
    for oi, c0, width, normed, scale, views in plan:
        for cc in range(0, width, COL_CHUNK):
            cw = min(COL_CHUNK, width - cc)
            y = jnp.dot(xn, w_ref[:, c0 + cc:c0 + cc + cw], preferred_element_type=F32)
            if normed:
                parts = []
                for n0 in range(0, cw, MXU_DIM):
                    nw = min(MXU_DIM, cw - n0)
                    yn = y[:, n0:n0 + nw]
                    hs = jnp.dot((yn * yn).astype(BF16), bd_ref[:nw, :nw], preferred_element_type=F32)
                    parts.append(yn * lax.rsqrt(hs * (1.0 / HEAD_DIM) + EPS)
                                 * hg_ref[:, c0 + cc + n0:c0 + cc + n0 + nw])
                y = parts[0] if len(parts) == 1 else jnp.concatenate(parts, axis=1)
            for j in range(cw // LANES):
                r = y[:, j * LANES:(j + 1) * LANES]
                if normed:
                    rot = jnp.where(first_half, pltpu.roll(r, LANES - HEAD_DIM // 2, 1),
                                    pltpu.roll(r, HEAD_DIM // 2, 1))
                    r = r * cos + rot * sin
                    if scale != 1.0:
                        r = r * scale
                cols = slice(cc + j * LANES, cc + (j + 1) * LANES)
                out_refs[oi][:, cols] = r.astype(BF16)
                if views:
                    stage_ref[cc // LANES + j] = r
        for d, vi in views:
            for rr in range(d):
                for j in range(width // LANES):
                    out_refs[vi][:, rr * width + j * LANES:rr * width + (j + 1) * LANES] = (
                        stage_ref[j, pl.ds(rr, tm // d, stride=d), :].astype(BF16))


def _inproj(x, g, w, hg, cos_t, sin_t, bd, plan, out_blocks, seq, stage_width=0):
    T, D = x.shape
    N = w.shape[1]
    tm = ROW_TILE
    n_pos_blocks = seq // tm
    const = lambda i: (0, 0)
    scratch = [pltpu.VMEM((stage_width // LANES, tm, LANES), F32)] if stage_width else []
    return pl.pallas_call(
        functools.partial(_inproj_kernel, plan=plan, n_out=len(out_blocks)),
        grid=(T // tm,),
        in_specs=[
            pl.BlockSpec((tm, D), lambda i: (i, 0)),
            pl.BlockSpec((1, D), const),
            pl.BlockSpec((D, N), const),
            pl.BlockSpec((1, N), const),
            pl.BlockSpec((tm, LANES), lambda i: (i % n_pos_blocks, 0)),
            pl.BlockSpec((tm, LANES), lambda i: (i % n_pos_blocks, 0)),
            pl.BlockSpec((MXU_DIM, MXU_DIM), const),
        ],
        out_specs=[pl.BlockSpec((br, bc), lambda i: (i, 0)) for br, bc in out_blocks],
        out_shape=[jax.ShapeDtypeStruct((T // tm * br, bc), BF16) for br, bc in out_blocks],
        scratch_shapes=scratch,
        compiler_params=_cparams(("parallel",)),
        name="inproj",
    )(x, g, w, hg, cos_t, sin_t, bd)


def _band_window(i, tq, hw, L):
    win = tq + 2 * hw
    start = jnp.clip(i * tq - hw, 0, L - win)
    start = pl.multiple_of(start, 64)
    qpos = i * tq + lax.broadcasted_iota(jnp.int32, (tq, win), 0)
    kpos = start + lax.broadcasted_iota(jnp.int32, (tq, win), 1)
    valid = jnp.abs(kpos - qpos) <= hw
    return start, win, valid


def _band_a_kernel(q_ref, k_ref, v_ref, o_ref, lse_ref, *, hw, tq, L, wb):
    lane = _lane_iota((1, LANES))
    head0 = lane < HEAD_DIM

    def qblock(i, carry):
        start, win, valid = _band_window(i, tq, hw, L)
        rows = pl.ds(pl.multiple_of(i * tq, tq), tq)
        for sl in range(wb // LANES):
            cols = slice(sl * LANES, (sl + 1) * LANES)
            q = q_ref[0, rows, cols]
            kw = k_ref[0, pl.ds(start, win), cols]
            vw = v_ref[0, pl.ds(start, win), cols]
            outs = []
            lses = []
            for hmask in (head0, jnp.logical_not(head0)):
                qh = jnp.where(hmask, q, jnp.zeros_like(q))
                s = lax.dot_general(qh, kw, (((1,), (1,)), ((), ())), preferred_element_type=F32)
                s = jnp.where(valid, s, NEG_INF)
                m = jnp.max(s, axis=-1, keepdims=True)
                p = jnp.exp(s - m)
                den = jnp.sum(p, axis=-1, keepdims=True)
                pv = jnp.dot(p.astype(BF16), vw, preferred_element_type=F32)
                outs.append(pv / den)
                lses.append(m + jnp.log(den))
            o_ref[0, rows, cols] = jnp.where(head0, outs[0], outs[1]).astype(o_ref.dtype)
            lse_ref[0, rows, cols] = jnp.where(head0, lses[0], lses[1])
        return carry

    lax.fori_loop(0, L // tq, qblock, 0, unroll=min(BAND_A_UNROLL, L // tq))


def _band_a(q, k, v, hw):
    NB, L, C = q.shape
    tq = BAND_A_TQ
    assert L >= tq + 2 * hw and L % tq == 0
    wb = min(C, A_HEADS * HEAD_DIM)
    while L * wb * 2 > BAND_BLOCK_BYTES and wb > LANES:
        wb //= 2
    spec = pl.BlockSpec((1, L, wb), lambda n, c: (n, 0, c))
    return pl.pallas_call(
        functools.partial(_band_a_kernel, hw=hw, tq=tq, L=L, wb=wb),
        grid=(NB, C // wb),
        in_specs=[spec, spec, spec],
        out_specs=[spec, spec],
        out_shape=[jax.ShapeDtypeStruct((NB, L, C), BF16), jax.ShapeDtypeStruct((NB, L, C), F32)],
        compiler_params=_cparams(("parallel", "parallel")),
        name="band_a",
    )(q, k, v)


def _band_core(i, L, hw, q_heads, k_ref, kcols, vT_scr, vslab, sink_row):
    tq = min(BAND_TQ, L)
    kb = BAND_KBLK
    nblk = L // kb
    wblk = min(tq // kb + 2, nblk)
    win = wblk * kb
    blk0 = jnp.clip(i * (tq // kb) - 1, 0, nblk - wblk)
    kw = k_ref[0, pl.ds(pl.multiple_of(blk0 * kb, kb), win), kcols]
    qq = jnp.concatenate(q_heads, axis=0)
    s = lax.dot_general(kw, qq, (((1,), (1,)), ((), ())), preferred_element_type=F32)
    kpos = blk0 * kb + lax.broadcasted_iota(jnp.int32, (win, tq), 0)
    qpos = i * tq + lax.broadcasted_iota(jnp.int32, (win, tq), 1)
    valid = jnp.abs(kpos - qpos) <= hw
    s = jnp.where(jnp.concatenate([valid] * len(q_heads), axis=1), s, NEG_INF)
    m = jnp.max(s, axis=0, keepdims=True)
    if sink_row is not None:
        m = jnp.maximum(m, sink_row)
    p = jnp.exp(s - m).astype(BF16)
    vT = jnp.concatenate([vT_scr[vslab, blk0 + w] for w in range(wblk)], axis=1)
    lhs = jnp.concatenate([vT, jnp.ones((BAND_ONES_ROWS, win), BF16)], axis=0)
    return jnp.dot(lhs, p, preferred_element_type=F32), m


def _transpose_v(v_ref, vT_scr, L, n_slabs):
    def blk_step(blk, carry):
        rows = pl.ds(pl.multiple_of(blk * BAND_KBLK, BAND_KBLK), BAND_KBLK)
        for sl in range(n_slabs):
            vb = v_ref[0, rows, sl * LANES:(sl + 1) * LANES].astype(F32)
            vT_scr[sl, blk] = vb.T.astype(BF16)
        return carry
    lax.fori_loop(0, L // BAND_KBLK, blk_step, 0)


def _band_b_kernel(sink_ref, q_ref, k_ref, v_ref, o_ref, vT_scr, *, hw, L):
    tq = min(BAND_TQ, L)
    lane = _lane_iota((1, LANES))
    head0 = lane < HEAD_DIM
    row0 = lax.broadcasted_iota(jnp.int32, (LANES, 1), 0) < HEAD_DIM
    hk = pl.program_id(1)
    group = B_Q_HEADS // B_KV_HEADS
    sink = jnp.concatenate(
        [jnp.full((1, tq), sink_ref[hk * group + g], F32) for g in range(group)], axis=1)
    _transpose_v(v_ref, vT_scr, L, 1)

    def qblock(i, carry):
        rows = pl.ds(pl.multiple_of(i * tq, tq), tq)
        qs = []
        for g in range(group):
            q = q_ref[0, rows, (g // 2) * LANES:(g // 2 + 1) * LANES]
            zero = jnp.zeros_like(q)
            qs.append(jnp.where(head0, q, zero) if g % 2 == 0 else jnp.where(head0, zero, q))
        acc, m = _band_core(i, L, hw, qs, k_ref, slice(0, LANES), vT_scr, 0, sink)
        den = acc[LANES:LANES + 1] + jnp.exp(sink - m)
        oT = acc[:LANES] / den
        for pair in range(group // 2):
            a = oT[:, (2 * pair) * tq:(2 * pair + 1) * tq]
            b = oT[:, (2 * pair + 1) * tq:(2 * pair + 2) * tq]
            o_ref[0, rows, pair * LANES:(pair + 1) * LANES] = jnp.where(row0, a, b).T.astype(o_ref.dtype)
        return carry

    lax.fori_loop(0, L // tq, qblock, 0, unroll=min(BAND_UNROLL, L // tq))


def _band_b(sink, q, kdup, vdup):
    B, S, C = q.shape
    tq = BAND_TQ
    hw = B_HALF_WINDOW
    qw = C // B_KV_HEADS
    assert hw <= BAND_KBLK and S % tq == 0
    return pl.pallas_call(
        functools.partial(_band_b_kernel, hw=hw, L=S),
        grid=(B, B_KV_HEADS),
        in_specs=[
            pl.BlockSpec(memory_space=pltpu.SMEM),
            pl.BlockSpec((1, S, qw), lambda b, h: (b, 0, h)),
            pl.BlockSpec((1, S, LANES), lambda b, h: (b, 0, h)),
            pl.BlockSpec((1, S, LANES), lambda b, h: (b, 0, h)),
        ],
        out_specs=pl.BlockSpec((1, S, qw), lambda b, h: (b, 0, h)),
        out_shape=jax.ShapeDtypeStruct((B, S, C), BF16),
        scratch_shapes=[pltpu.VMEM((1, S // BAND_KBLK, LANES, BAND_KBLK), BF16)],
        compiler_params=_cparams(("parallel", "parallel")),
        name="band_b",
    )(sink, q, kdup, vdup)


def _outproj_even_kernel(o1_ref, o4_ref, o16_ref, l1_ref, l4_ref, l16_ref, ob_ref, w_ref, x_ref, y_ref,
                         so4, sl4, so16, sl16):
    tm, half = o1_ref.shape

    def token_major(src_ref, stage_ref, d):
        for rr in range(d):
            for j in range(half // LANES):
                stage_ref[j, pl.ds(rr, tm // d, stride=d), :] = (
                    src_ref[:, rr * half + j * LANES:rr * half + (j + 1) * LANES].astype(F32))
        return jnp.concatenate([stage_ref[j] for j in range(half // LANES)], axis=1)

    dils = [d for _, d in DILATION_GROUPS]
    l1 = l1_ref[...]
    o4, l4 = token_major(o4_ref, so4, dils[1]), token_major(l4_ref, sl4, dils[1])
    o16, l16 = token_major(o16_ref, so16, dils[2]), token_major(l16_ref, sl16, dils[2])
    m = jnp.maximum(jnp.maximum(l1, l4), l16)
    e1, e4, e16 = jnp.exp(l1 - m), jnp.exp(l4 - m), jnp.exp(l16 - m)
    num = e1 * o1_ref[...].astype(F32) + e4 * o4 + e16 * o16
    oa = (num / (e1 + e4 + e16)).astype(BF16)
    ka = oa.shape[1]
    acc = jnp.dot(oa, w_ref[:ka, :], preferred_element_type=F32)
    acc = acc + jnp.dot(ob_ref[...], w_ref[ka:, :], preferred_element_type=F32)
    y_ref[...] = x_ref[...] + acc


def _outproj_even(o1, o4, o16, l1, l4, l16, ob, w, x):
    T, D = x.shape
    tm = ROW_TILE
    half = o1.shape[1]
    row = lambda wd: pl.BlockSpec((tm, wd), lambda i: (i, 0))
    view = lambda d: pl.BlockSpec((tm // d, d * half), lambda i: (i, 0))
    d4, d16 = DILATION_GROUPS[1][1], DILATION_GROUPS[2][1]
    return pl.pallas_call(
        _outproj_even_kernel,
        grid=(T // tm,),
        in_specs=[row(half), view(d4), view(d16), row(half), view(d4), view(d16), row(half),
                  pl.BlockSpec(w.shape, lambda i: (0, 0)), row(D)],
        out_specs=row(D),
        out_shape=jax.ShapeDtypeStruct((T, D), F32),
        scratch_shapes=[pltpu.VMEM((half // LANES, tm, LANES), F32)] * 4,
        compiler_params=_cparams(("parallel",)),
        name="outproj_even",
    )(o1, o4, o16, l1, l4, l16, ob, w, x)


def _outproj_kernel(o_ref, w_ref, x_ref, y_ref):
    y_ref[...] = x_ref[...] + jnp.dot(o_ref[...], w_ref[...], preferred_element_type=F32)


def _outproj(o, w, x):
    T, D = x.shape
    tm = ROW_TILE
    row = lambda wd: pl.BlockSpec((tm, wd), lambda i: (i, 0))
    return pl.pallas_call(
        _outproj_kernel,
        grid=(T // tm,),
        in_specs=[row(o.shape[1]), pl.BlockSpec(w.shape, lambda i: (0, 0)), row(D)],
        out_specs=row(D),
        out_shape=jax.ShapeDtypeStruct((T, D), F32),
        compiler_params=_cparams(("parallel",)),
        name="outproj",
    )(o, w, x)


def _diff_kernel(lam_ref, qT_ref, k_ref, vT_ref, sub_ref, o_ref, s0_scr, s1_scr, acc_scr, *,
                 tq, tk, S, lam_init, unroll):
    row = lax.broadcasted_iota(jnp.int32, (LANES, 1), 0)
    comp0 = row < HEAD_DIM
    lamv = lam_ref[...]
    lam = (jnp.exp(jnp.sum(lamv[0:1] * lamv[1:2], axis=-1, keepdims=True))
           - jnp.exp(jnp.sum(lamv[2:3] * lamv[3:4], axis=-1, keepdims=True)) + lam_init)
    qT = qT_ref[0, 0]
    zero = jnp.zeros_like(qT)
    qqT = jnp.concatenate([jnp.where(comp0, qT, zero), jnp.where(comp0, zero, qT)], axis=1)
    ones = jnp.ones((DIFF_ONES_ROWS, tk), BF16)
    n_kv = S // tk

    def scores(t, dst):
        kb = k_ref[0, pl.ds(pl.multiple_of(t * tk, tk), tk), :]
        dst[...] = jnp.dot(kb, qqT, preferred_element_type=F32)

    def softmax_pv(src, t, m):
        s = src[...]
        lhs = jnp.concatenate([vT_ref[0, 0, t], ones], axis=0)
        m_new = jnp.maximum(m, jnp.max(s, axis=0, keepdims=True))
        alpha = jnp.exp2(m - m_new)
        p = jnp.exp2(s - m_new).astype(BF16)
        acc_scr[...] = alpha * acc_scr[...] + jnp.dot(lhs, p, preferred_element_type=F32)
        return m_new

    def step(i, m):
        bufs = (s0_scr, s1_scr)
        for u in range(unroll):
            t = unroll * i + u
            scores(jnp.minimum(t + 1, n_kv - 1), bufs[(u + 1) % 2])
            m = softmax_pv(bufs[u % 2], t, m)
        return m

    acc_scr[...] = jnp.zeros_like(acc_scr)
    scores(0, s0_scr)
    lax.fori_loop(0, n_kv // unroll, step, jnp.full((1, 2 * tq), NEG_INF, F32))
    acc = acc_scr[...]
    a = acc[:LANES] / acc[LANES:LANES + 1]
    o = a[:, :tq] - lam * a[:, tq:]
    ms = jnp.mean(o * o, axis=0, keepdims=True)
    o = o * lax.rsqrt(ms + EPS) * sub_ref[...] * (1.0 - lam_init)
    o_ref[0] = o.T.astype(o_ref.dtype)


def _diff_attention(lam_vecs, qT, k, vT, subln_col, lam_init):
    B, H, _, S = qT.shape
    tq, tk = DIFF_TQ, DIFF_TK
    unroll = min(DIFF_UNROLL, S // tk)
    assert S % tq == 0 and S % (tk * unroll) == 0 and unroll % 2 == 0
    acc_rows = LANES + DIFF_ONES_ROWS
    return pl.pallas_call(
        functools.partial(_diff_kernel, tq=tq, tk=tk, S=S, lam_init=lam_init, unroll=unroll),
        grid=(B, H, S // tq),
        in_specs=[
            pl.BlockSpec(lam_vecs.shape, lambda b, h, i: (0, 0)),
            pl.BlockSpec((1, 1, LANES, tq), lambda b, h, i: (b, h, 0, i)),
            pl.BlockSpec((1, S, LANES), lambda b, h, i: (b, 0, h)),
            pl.BlockSpec((1, 1, S // tk, LANES, tk), lambda b, h, i: (b, h, 0, 0, 0)),
            pl.BlockSpec((LANES, 1), lambda b, h, i: (0, 0)),
        ],
        out_specs=pl.BlockSpec((1, tq, LANES), lambda b, h, i: (b, i, h)),
        out_shape=jax.ShapeDtypeStruct((B, S, H * LANES), BF16),
        scratch_shapes=[pltpu.VMEM((tk, 2 * tq), F32), pltpu.VMEM((tk, 2 * tq), F32),
                        pltpu.VMEM((acc_rows, 2 * tq), F32)],
        compiler_params=_cparams(("parallel", "parallel", "arbitrary")),
        name="diff_attn",
    )(lam_vecs, qT, k, vT, subln_col)


def _router_kernel(x_ref, g_ref, whi_ref, wlo_ref, b_ref, xn_ref, gate_ref, eid_ref, cnt_ref, *, n_token_tiles):
    i = pl.program_id(0)

    @pl.when(i < n_token_tiles)
    def _():
        _router_tile(x_ref, g_ref, whi_ref, wlo_ref, b_ref, xn_ref, gate_ref, eid_ref, cnt_ref)

    @pl.when(i >= n_token_tiles)
    def _():
        xn_ref[...] = jnp.zeros_like(xn_ref)


def _router_tile(x_ref, g_ref, whi_ref, wlo_ref, b_ref, xn_ref, gate_ref, eid_ref, cnt_ref):
    x = x_ref[...]
    ms = jnp.mean(x * x, axis=-1, keepdims=True)
    xn = x * lax.rsqrt(ms + EPS) * g_ref[...]
    xhi = xn.astype(BF16)
    half = xn.shape[1] // 2
    hi_bits = lax.bitcast_convert_type(xhi[:, :half].astype(F32), jnp.uint32)
    lo_bits = lax.bitcast_convert_type(xhi[:, half:].astype(F32), jnp.uint32)
    xn_ref[...] = hi_bits | (lo_bits >> 16)
    xlo = (xn - xhi.astype(F32)).astype(BF16)
    whi = whi_ref[...]
    logits = (jnp.dot(xhi, whi, preferred_element_type=F32)
              + jnp.dot(xlo, whi, preferred_element_type=F32)
              + jnp.dot(xhi, wlo_ref[...], preferred_element_type=F32)) + b_ref[...]
    lane = _lane_iota(logits.shape).astype(F32)
    big = float(4 * LANES)
    lg = jnp.where(lane < N_GROUPS, logits, NEG_INF)
    mg = jnp.max(lg, axis=-1, keepdims=True)
    g_sel = jnp.min(jnp.where(lg == mg, lane, big), axis=-1, keepdims=True)
    p_sel = 1.0 / jnp.sum(jnp.exp(lg - mg), axis=-1, keepdims=True)
    e_lo = ROUTER_E0 + g_sel * EXPERTS_PER_GROUP
    le = jnp.where((lane >= e_lo) & (lane < e_lo + EXPERTS_PER_GROUP), logits, NEG_INF)
    v1 = jnp.max(le, axis=-1, keepdims=True)
    i1 = jnp.min(jnp.where(le == v1, lane, big), axis=-1, keepdims=True)
    le2 = jnp.where(lane == i1, NEG_INF, le)
    v2 = jnp.max(le2, axis=-1, keepdims=True)
    i2 = jnp.min(jnp.where(le2 == v2, lane, big), axis=-1, keepdims=True)
    e2 = jnp.exp(v2 - v1)
    den = 1.0 + e2
    g1 = (1.0 / den) * p_sel
    g2 = (e2 / den) * p_sel
    gate_ref[...] = jnp.where(lane == 0, g1, jnp.where(lane == 1, g2, 0.0))
    eid = jnp.where(lane == 0, i1 - ROUTER_E0, jnp.where(lane == 1, i2 - ROUTER_E0, 0.0))
    eid_ref[...] = eid.astype(jnp.int32)
    chosen = ((lane == i1) | (lane == i2)).astype(F32)

    @pl.when(pl.program_id(0) == 0)
    def _():
        cnt_ref[...] = jnp.zeros_like(cnt_ref)

    cnt_ref[...] += jnp.sum(chosen, axis=0, keepdims=True)


def _router(x, g, whi, wlo, bias):
    T, D = x.shape
    tm = ROW_TILE
    n_tok = T // tm
    table_rows = max(T, DISPATCH_TABLE_ROWS)
    const = lambda i: (0, 0)
    row = lambda wd: pl.BlockSpec((tm, wd), lambda i: (jnp.minimum(i, n_tok - 1), 0))
    return pl.pallas_call(
        functools.partial(_router_kernel, n_token_tiles=n_tok),
        grid=(table_rows // tm,),
        in_specs=[row(D), pl.BlockSpec((1, D), const), pl.BlockSpec((D, ROUTER_COLS), const),
                  pl.BlockSpec((D, ROUTER_COLS), const), pl.BlockSpec((1, ROUTER_COLS), const)],
        out_specs=[pl.BlockSpec((tm, D // 2), lambda i: (i, 0)), row(ROUTER_COLS), row(ROUTER_COLS),
                   pl.BlockSpec((1, ROUTER_COLS), const)],
        out_shape=[jax.ShapeDtypeStruct((table_rows, D // 2), jnp.uint32),
                   jax.ShapeDtypeStruct((T, ROUTER_COLS), F32),
                   jax.ShapeDtypeStruct((T, ROUTER_COLS), jnp.int32),
                   jax.ShapeDtypeStruct((1, ROUTER_COLS), F32)],
        compiler_params=_cparams(("arbitrary",)),
        name="router",
    )(x, g, whi, wlo, bias)


def _slot_kernel(eid_ref, pstart_ref, slot_ref, run_ref):
    tm = eid_ref.shape[0]

    @pl.when(pl.program_id(0) == 0)
    def _():
        run_ref[...] = jnp.zeros_like(run_ref)

    eid = eid_ref[...]
    lane = _lane_iota(eid.shape)
    oh0 = lane == (eid[:, 0:1] + ROUTER_E0)
    oh1 = lane == (eid[:, 1:2] + ROUTER_E0)
    both = (oh0 | oh1).astype(BF16)
    r = lax.broadcasted_iota(jnp.int32, (tm, tm), 0)
    c = lax.broadcasted_iota(jnp.int32, (tm, tm), 1)
    earlier = (c < r).astype(BF16)
    before = jnp.dot(earlier, both, preferred_element_type=F32)
    base = pstart_ref[...] + run_ref[...] + before
    s0 = jnp.sum(jnp.where(oh0, base, 0.0), axis=-1, keepdims=True)
    s1 = jnp.sum(jnp.where(oh1, base, 0.0), axis=-1, keepdims=True)
    slot_ref[...] = jnp.where(lane == 0, s0, jnp.where(lane == 1, s1, 0.0)).astype(jnp.int32)
    run_ref[...] += jnp.sum(both.astype(F32), axis=0, keepdims=True)


def _slots(eid, pstart_row):
    T = eid.shape[0]
    tm = ROW_TILE
    return pl.pallas_call(
        _slot_kernel,
        grid=(T // tm,),
        in_specs=[pl.BlockSpec((tm, ROUTER_COLS), lambda i: (i, 0)),
                  pl.BlockSpec((1, ROUTER_COLS), lambda i: (0, 0))],
        out_specs=pl.BlockSpec((tm, ROUTER_COLS), lambda i: (i, 0)),
        out_shape=jax.ShapeDtypeStruct((T, ROUTER_COLS), jnp.int32),
        scratch_shapes=[pltpu.VMEM((1, ROUTER_COLS), F32)],
        compiler_params=_cparams(("arbitrary",)),
        name="slots",
    )(eid, pstart_row)


def _expert_kernel(te_ref, nt_ref, nxt_ref, par_ref, xs_ref, wg_hbm, wu_hbm, wd_hbm, y_ref,
                   wg_f, wu_f, wd_f, wg_s, wu_s, wd_s, sem, *, layer):
    t = pl.program_id(0)
    live = t < nt_ref[0]
    e = te_ref[t]
    new_expert = (t == 0) | (e != te_ref[jnp.maximum(t - 1, 0)])
    slot = par_ref[t]

    def weight_copies(expert, s):
        return (pltpu.make_async_copy(wg_hbm.at[layer, expert], wg_f.at[s], sem.at[s, 0]),
                pltpu.make_async_copy(wu_hbm.at[layer, expert], wu_f.at[s], sem.at[s, 1]),
                pltpu.make_async_copy(wd_hbm.at[layer, expert], wd_f.at[s], sem.at[s, 2]))

    @pl.when(live & (t == 0))
    def _():
        for c in weight_copies(e, slot):
            c.start()

    @pl.when(live & new_expert)
    def _():
        for c in weight_copies(e, slot):
            c.wait()
        wg_s[...] = wg_f[slot].astype(BF16)
        wu_s[...] = wu_f[slot].astype(BF16)
        wd_s[...] = wd_f[slot].astype(BF16)
        nxt = nxt_ref[t]

        @pl.when(nxt >= 0)
        def _():
            for c in weight_copies(nxt, 1 - slot):
                c.start()

    @pl.when(live)
    def _():
        words = xs_ref[...]
        hi = lax.bitcast_convert_type(words & jnp.uint32(0xFFFF0000), F32).astype(BF16)
        lo = lax.bitcast_convert_type(words << 16, F32).astype(BF16)
        xs = jnp.concatenate([hi, lo], axis=1)
        a = jnp.dot(xs, wg_s[...], preferred_element_type=F32)
        u = jnp.dot(xs, wu_s[...], preferred_element_type=F32)
        h = (a * jax.nn.sigmoid(a) * u).astype(BF16)
        y_ref[...] = jnp.dot(h, wd_s[...], preferred_element_type=F32).astype(y_ref.dtype)

    @pl.when(jnp.logical_not(live))
    def _():
        y_ref[...] = jnp.zeros_like(y_ref)


def _experts(tile_expert, n_tiles, next_expert, parity, xs, wg, wu, wd, layer):
    Mp = xs.shape[0]
    D = wg.shape[2]
    te = EXPERT_TILE
    grid_spec = pltpu.PrefetchScalarGridSpec(
        num_scalar_prefetch=4,
        grid=(Mp // te,),
        in_specs=[
            pl.BlockSpec((te, D // 2), lambda t, *_: (t, 0)),
            pl.BlockSpec(memory_space=pl.ANY),
            pl.BlockSpec(memory_space=pl.ANY),
            pl.BlockSpec(memory_space=pl.ANY),
        ],
        out_specs=pl.BlockSpec((te, D), lambda t, *_: (t, 0)),
        scratch_shapes=[pltpu.VMEM((2, D, D_EXPERT), F32), pltpu.VMEM((2, D, D_EXPERT), F32),
                        pltpu.VMEM((2, D_EXPERT, D), F32),
                        pltpu.VMEM((D, D_EXPERT), BF16), pltpu.VMEM((D, D_EXPERT), BF16),
                        pltpu.VMEM((D_EXPERT, D), BF16),
                        pltpu.SemaphoreType.DMA((2, 3))],
    )
    return pl.pallas_call(
        functools.partial(_expert_kernel, layer=layer),
        grid_spec=grid_spec,
        out_shape=jax.ShapeDtypeStruct((Mp, D), BF16),
        compiler_params=_cparams(("arbitrary",)),
        name="experts",
    )(tile_expert, n_tiles, next_expert, parity, xs, wg, wu, wd)


def _combine_kernel(x_ref, y0_ref, y1_ref, gate_ref, o_ref):
    gate = gate_ref[...]
    o_ref[...] = (x_ref[...] + gate[:, 0:1] * y0_ref[...].astype(F32)
                  + gate[:, 1:2] * y1_ref[...].astype(F32))


def _combine(x, y0, y1, gate):
    T, D = x.shape
    tm = ROW_TILE
    row = lambda wd: pl.BlockSpec((tm, wd), lambda i: (i, 0))
    return pl.pallas_call(
        _combine_kernel,
        grid=(T // tm,),
        in_specs=[row(D), row(D), row(D), row(ROUTER_COLS)],
        out_specs=row(D),
        out_shape=jax.ShapeDtypeStruct((T, D), F32),
        compiler_params=_cparams(("parallel",)),
        name="moe_combine",
    )(x, y0, y1, gate)


def _rows(table, idx):
    return table.at[idx].get(mode="promise_in_bounds")


def _moe_route(x, p):
    T, D = x.shape
    xn, gate, eid, counts_row = _router(x, p["ffn_norm"], p["r_whi"], p["r_wlo"], p["r_bias"])
    M = T * TOP_K
    te = EXPERT_TILE
    Mp = M + N_EXPERTS * te
    counts = counts_row[0, ROUTER_E0:ROUTER_E0 + N_EXPERTS].astype(jnp.int32)
    padded = ((counts + te - 1) // te) * te
    pend = jnp.cumsum(padded)
    pstart = pend - padded
    pstart_row = jnp.zeros((1, ROUTER_COLS), F32).at[0, ROUTER_E0:ROUTER_E0 + N_EXPERTS].set(pstart.astype(F32))
    slot = _slots(eid, pstart_row)[:, :TOP_K]
    tok_of_slot = jnp.zeros((Mp,), jnp.int32).at[slot.reshape(M)].set(
        jnp.arange(M, dtype=jnp.int32) // TOP_K, mode="promise_in_bounds", unique_indices=True)
    tile_ids = jnp.arange(Mp // te, dtype=jnp.int32)
    tile_expert = jnp.minimum(jnp.sum((pend // te)[None, :] <= tile_ids[:, None], axis=1),
                              N_EXPERTS - 1).astype(jnp.int32)
    n_tiles = (pend[-1] // te).astype(jnp.int32).reshape(1)
    present = counts > 0
    ids = jnp.arange(N_EXPERTS, dtype=jnp.int32)
    later = (ids[None, :] > ids[:, None]) & present[None, :]
    next_present = jnp.min(jnp.where(later, ids[None, :], N_EXPERTS), axis=1)
    next_present = jnp.where(next_present == N_EXPERTS, -1, next_present).astype(jnp.int32)
    rank = (jnp.cumsum(present.astype(jnp.int32)) - present.astype(jnp.int32))
    return dict(x=x, xn=xn, gate=gate, slot=slot, tok_of_slot=tok_of_slot, tile_expert=tile_expert,
                n_tiles=n_tiles, next_expert=next_present[tile_expert],
                parity=(rank[tile_expert] % 2).astype(jnp.int32))


def _moe_experts(r, xs, p, layer):
    return _experts(r["tile_expert"], r["n_tiles"], r["next_expert"], r["parity"], xs,
                    p["w_gate"], p["w_up"], p["w_down"], layer)


def _moe(x, p, layer):
    r = _moe_route(x, p)
    ys = _moe_experts(r, _rows(r["xn"], r["tok_of_slot"]), p, layer)
    return _combine(x, _rows(ys, r["slot"][:, 0]), _rows(ys, r["slot"][:, 1]), r["gate"])


def _rope_tables(seq):
    half = HEAD_DIM // 2
    inv = ROPE_THETA ** (-jnp.arange(half, dtype=F32) / half)
    ang = jnp.arange(seq, dtype=F32)[:, None] * inv[None, :]
    cos = jnp.tile(jnp.cos(ang), (1, LANES // half))
    sin = jnp.sin(ang)
    sin = jnp.tile(jnp.concatenate([-sin, sin], axis=1), (1, LANES // HEAD_DIM))
    return cos, sin


def _head_gain(g, heads):
    return jnp.tile(g.astype(F32), (heads,))


def _prepare(ev_norm, ev_w_in, ev_qn_a, ev_kn_a, ev_qn_b, ev_kn_b, ev_sink, ev_w_out,
             od_norm, od_w_in, od_qn, od_kn, od_lq1, od_lk1, od_lq2, od_lk2, od_subln, od_w_out,
             ffn_norm, rg_w, rg_b, re_w, re_b, w_gate, w_up, w_down):
    hd = HEAD_DIM
    P = {}
    r = jnp.arange(MXU_DIM) // hd
    P["bd"] = (r[:, None] == r[None, :]).astype(BF16)
    w = ev_w_in[0]
    a3 = 3 * A_HEADS * hd
    qb_end = a3 + B_Q_HEADS * hd
    kb = [w[:, qb_end + h * hd:qb_end + (h + 1) * hd] for h in range(B_KV_HEADS)]
    vb0 = qb_end + B_KV_HEADS * hd
    vb = [w[:, vb0 + h * hd:vb0 + (h + 1) * hd] for h in range(B_KV_HEADS)]
    P["ev_w_in"] = jnp.concatenate([w[:, :qb_end]] + [c for h in kb for c in (h, h)]
                                   + [c for h in vb for c in (h, h)], axis=1).astype(BF16)
    ones = jnp.ones((A_HEADS * hd,), F32)
    P["ev_hg"] = jnp.concatenate([
        _head_gain(ev_qn_a[0], A_HEADS), _head_gain(ev_kn_a[0], A_HEADS), ones,
        _head_gain(ev_qn_b[0], B_Q_HEADS), _head_gain(ev_kn_b[0], 2 * B_KV_HEADS),
        jnp.ones((2 * B_KV_HEADS * hd,), F32)])[None, :]
    P["ev_norm"] = ev_norm[0][None, :]
    P["ev_sink"] = ev_sink[0].astype(F32)
    P["ev_w_out"] = ev_w_out[0].astype(BF16)
    P["od_w_in"] = od_w_in[0].astype(BF16)
    P["od_hg"] = jnp.concatenate([_head_gain(od_qn[0], 2 * C_HEADS), _head_gain(od_kn[0], 2 * C_HEADS),
                                  jnp.ones((2 * C_HEADS * hd,), F32)])[None, :]
    P["od_norm"] = od_norm[0][None, :]
    P["od_lam"] = jnp.stack([od_lq1[0], od_lk1[0], od_lq2[0], od_lk2[0]]).astype(F32)
    P["od_subln"] = od_subln[0][:, None].astype(F32)
    P["od_w_out"] = od_w_out[0].astype(BF16)
    P["moe"] = []
    for layer in range(ffn_norm.shape[0]):
        wr = jnp.zeros((D_MODEL, ROUTER_COLS), F32)
        wr = wr.at[:, :N_GROUPS].set(rg_w[layer])
        we = jnp.transpose(re_w[layer], (1, 0, 2)).reshape(D_MODEL, N_EXPERTS)
        wr = wr.at[:, ROUTER_E0:ROUTER_E0 + N_EXPERTS].set(we)
        bias = jnp.zeros((ROUTER_COLS,), F32).at[:N_GROUPS].set(rg_b[layer])
        bias = bias.at[ROUTER_E0:ROUTER_E0 + N_EXPERTS].set(re_b[layer].reshape(N_EXPERTS))
        whi = wr.astype(BF16)
        P["moe"].append({
            "ffn_norm": ffn_norm[layer][None, :],
            "r_whi": whi, "r_wlo": (wr - whi.astype(F32)).astype(BF16), "r_bias": bias[None, :],
            "w_gate": w_gate, "w_up": w_up, "w_down": w_down,
        })
    return P


def _even_mixer(x, P, B, S):
    T = B * S
    hd = HEAD_DIM
    aw = A_HEADS * hd
    cos_t, sin_t = _rope_tables(S)
    scale = hd ** -0.5
    tm = ROW_TILE
    dils = [d for _, d in DILATION_GROUPS]
    assert dils[0] == 1
    out_blocks = [(tm, aw)] * 4 + [(tm, 2 * LANES)] * 2
    views = []
    for _ in range(3):
        tv = []
        for d in dils[1:]:
            tv.append((d, len(out_blocks)))
            out_blocks.append((tm // d, d * aw))
        views.append(tuple(tv))
    plan = ((0, 0, aw, True, scale, views[0]), (1, aw, aw, True, 1.0, views[1]),
            (2, 2 * aw, aw, False, 1.0, views[2]), (3, 3 * aw, aw, True, scale, ()),
            (4, 4 * aw, 2 * LANES, True, 1.0, ()), (5, 4 * aw + 2 * LANES, 2 * LANES, False, 1.0, ()))
    res = _inproj(x, P["ev_norm"], P["ev_w_in"], P["ev_hg"], cos_t, sin_t, P["bd"], plan, out_blocks, S,
                  stage_width=aw)
    qb, kb, vb = res[3:6]
    outs, lses = [], []
    for gi, (window, dil) in enumerate(DILATION_GROUPS):
        L = S // dil
        qkv = [res[t] if dil == 1 else res[views[t][gi - 1][1]] for t in range(3)]
        o, lse = _band_a(*[t.reshape(B, L, dil * aw) for t in qkv], window // (2 * dil))
        outs.append(o.reshape(T // dil, dil * aw))
        lses.append(lse.reshape(T // dil, dil * aw))
    ob = _band_b(P["ev_sink"], qb.reshape(B, S, aw), kb.reshape(B, S, 2 * LANES), vb.reshape(B, S, 2 * LANES))
    return _outproj_even(outs[0], outs[1], outs[2], lses[0], lses[1], lses[2], ob.reshape(T, aw),
                         P["ev_w_out"], x)


def _diff_mixer(x, P, B, S, lam_init):
    T = B * S
    hd = HEAD_DIM
    cw = 2 * C_HEADS * hd
    cos_t, sin_t = _rope_tables(S)
    plan = ((0, 0, cw, True, LOG2_E * hd ** -0.5, ()), (1, cw, cw, True, 1.0, ()), (2, 2 * cw, cw, False, 1.0, ()))
    q, k, v = _inproj(x, P["od_norm"], P["od_w_in"], P["od_hg"], cos_t, sin_t, P["bd"], plan,
                      [(ROW_TILE, cw)] * 3, S)
    tk = DIFF_TK
    qT = q.reshape(B, S, C_HEADS, LANES).transpose(0, 2, 3, 1)
    vT = v.reshape(B, S // tk, tk, C_HEADS, LANES).transpose(0, 3, 1, 4, 2)
    o = _diff_attention(P["od_lam"], qT, k.reshape(B, S, cw), vT, P["od_subln"], lam_init)
    return _outproj(o.reshape(T, cw), P["od_w_out"], x)


def _flat(x3):
    B, S, D = x3.shape
    return x3.reshape(B * S, D)


def kernel(x_prompt, x_sample, ev_norm, ev_w_in, ev_qn_a, ev_kn_a, ev_qn_b, ev_kn_b, ev_sink, ev_w_out,
           od_norm, od_w_in, od_qn, od_kn, od_lq1, od_lk1, od_lq2, od_lk2, od_subln, od_w_out,
           ffn_norm, rg_w, rg_b, re_w, re_b, w_gate, w_up, w_down):
    P = _prepare(ev_norm, ev_w_in, ev_qn_a, ev_kn_a, ev_qn_b, ev_kn_b, ev_sink, ev_w_out,
                 od_norm, od_w_in, od_qn, od_kn, od_lq1, od_lk1, od_lq2, od_lk2, od_subln, od_w_out,
                 ffn_norm, rg_w, rg_b, re_w, re_b, w_gate, w_up, w_down)
    lam_init = 0.8 - 0.6 * math.exp(-0.3 * 1)
    (Ba, Sa, D), (Bb, Sb, _) = x_prompt.shape, x_sample.shape
    moe = P["moe"]

    def tie(first, second):
        return lax.optimization_barrier((first, second))

    def moe_overlapped(r, layer, big_other, route_other):
        xs = _rows(r["xn"], r["tok_of_slot"])
        other_x = big_other()
        xs, other_x = tie(xs, other_x)
        ys = _moe_experts(r, xs, moe[layer], layer)
        y0, y1 = _rows(ys, r["slot"][:, 0]), _rows(ys, r["slot"][:, 1])
        other_r = route_other(other_x)
        if other_r is not None:
            (y0, y1), other_r = tie((y0, y1), other_r)
        return _combine(r["x"], y0, y1, r["gate"]), other_r

    xa = _even_mixer(_flat(x_prompt), P, Ba, Sa)
    ra = _moe_route(xa, moe[0])
    xa, rb = moe_overlapped(ra, 0, lambda: _even_mixer(_flat(x_sample), P, Bb, Sb),
                            lambda xb: _moe_route(xb, moe[0]))
    xb, ra = moe_overlapped(rb, 0, lambda: _diff_mixer(xa, P, Ba, Sa, lam_init),
                            lambda xa1: _moe_route(xa1, moe[1]))
    xs_a = _rows(ra["xn"], ra["tok_of_slot"])
    xb = _diff_mixer(xb, P, Bb, Sb, lam_init)
    xs_a, xb = tie(xs_a, xb)
    rb = _moe_route(xb, moe[1])
    xs_a, tok_b = tie(xs_a, rb["tok_of_slot"])
    xs_b = _rows(rb["xn"], tok_b)
    ys_a = _moe_experts(ra, xs_a, moe[1], 1)
    ya = (_rows(ys_a, ra["slot"][:, 0]), _rows(ys_a, ra["slot"][:, 1]))
    xs_b, ys_a = tie(xs_b, ys_a)
    ys_b = _moe_experts(rb, xs_b, moe[1], 1)
    ya, ys_b = tie(ya, ys_b)
    xa = _combine(ra["x"], ya[0], ya[1], ra["gate"])
    xb = _combine(rb["x"], _rows(ys_b, rb["slot"][:, 0]), _rows(ys_b, rb["slot"][:, 1]), rb["gate"])
    return (xa.reshape(Ba, Sa, D), xb.reshape(Bb, Sb, D))
```
